```python
import math, functools
import jax, jax.numpy as jnp
from jax import lax
import numpy as np


D_MODEL = 2048
BATCH = 4
SEQ = 4096
DEPTH = 4

N_MIXERS = 3
GRID_W = 64
ROPE_BASE = 10000.0
NORM_EPS = 1e-6
GN_EPS = 1e-5

RET_HEADS = 8
RET_DK = D_MODEL // RET_HEADS
RET_DV = 2 * RET_DK
RET_CHUNK = 128

NA_HEADS = D_MODEL // 128
NA_DH = D_MODEL // NA_HEADS
NA_WIN_R = 8
NA_WIN_C = 16

MLA_HEADS = D_MODEL // 128
MLA_Q_RANK = 512
MLA_KV_RANK = 512
MLA_NOPE = 128
MLA_ROPE = 64
MLA_V = 128
MLA_QBLOCK = 128

D_FF = 11 * D_MODEL // 4
CONV_W = 3

kernel_name = "hybrid_retention_natten_mla_convffn_encoder"


def rmsnorm(x, g):
    xf = x.astype(jnp.float32)
    y = xf * lax.rsqrt(jnp.mean(jnp.square(xf), axis=-1, keepdims=True) + NORM_EPS)
    return (y * g.astype(jnp.float32)).astype(x.dtype)


def rope(x, pos):
    d = x.shape[-1]
    half = d // 2
    inv = ROPE_BASE ** (-jnp.arange(half, dtype=jnp.float32) * 2.0 / d)
    ang = pos[:, None] * inv[None, :]
    cos, sin = jnp.cos(ang), jnp.sin(ang)
    xf = x.astype(jnp.float32)
    x1, x2 = xf[..., :half], xf[..., half:]
    return jnp.concatenate([x1 * cos - x2 * sin, x1 * sin + x2 * cos], axis=-1).astype(x.dtype)


def retention_dir(q, k, v, log_gamma, strict):
    B, H, T, dk = q.shape
    dv = v.shape[-1]
    C = RET_CHUNK
    N = T // C
    idx = jnp.arange(C, dtype=jnp.float32)
    diff = idx[:, None] - idx[None, :]
    mask = (diff > 0) if strict else (diff >= 0)
    lg = log_gamma[:, None, None]
    intra_decay = jnp.where(mask, jnp.exp(jnp.where(mask, diff, 0.0) * lg), 0.0)
    q_decay = jnp.exp((idx + 1.0) * log_gamma[:, None])[..., None]
    k_decay = jnp.exp((C - 1.0 - idx) * log_gamma[:, None])[..., None]
    chunk_decay = jnp.exp(C * log_gamma)[:, None, None]

    def to_chunks(a):
        return jnp.moveaxis(a.reshape(B, H, N, C, a.shape[-1]), 2, 0)

    def step(state, qkv):
        qc, kc, vc = qkv
        scores = jnp.einsum('bhid,bhjd->bhij', qc, kc) * intra_decay
        out = (jnp.einsum('bhij,bhjv->bhiv', scores, vc)
               + jnp.einsum('bhid,bhdv->bhiv', qc * q_decay, state))
        state = state * chunk_decay + jnp.einsum('bhjd,bhjv->bhdv', kc * k_decay, vc)
        return state, out

    state0 = jnp.zeros((B, H, dk, dv), jnp.float32)
    _, out = lax.scan(step, state0, (to_chunks(q), to_chunks(k), to_chunks(v)))
    return jnp.moveaxis(out, 0, 2).reshape(B, H, T, dv)


def retention_mixer(h, w_in, decay_fwd, decay_bwd, w_out, pos):
    B, T, _ = h.shape
    qk_w = RET_HEADS * RET_DK
    v_w = RET_HEADS * RET_DV
    proj = h @ w_in

    def heads(a, d):
        return a.reshape(B, T, RET_HEADS, d).transpose(0, 2, 1, 3).astype(jnp.float32)

    q = rope(heads(proj[..., :qk_w], RET_DK), pos)
    k = rope(heads(proj[..., qk_w:2 * qk_w], RET_DK), pos) * (RET_DK ** -0.5)
    v = heads(proj[..., 2 * qk_w:2 * qk_w + v_w], RET_DV)
    g = proj[..., 2 * qk_w + v_w:]
    lg_f = jax.nn.log_sigmoid(decay_fwd.astype(jnp.float32))
    lg_b = jax.nn.log_sigmoid(decay_bwd.astype(jnp.float32))
    o_f = retention_dir(q, k, v, lg_f, False)
    flip = lambda a: jnp.flip(a, axis=2)
    o_b = flip(retention_dir(flip(q), flip(k), flip(v), lg_b, True))
    o = o_f + o_b
    mu = jnp.mean(o, axis=-1, keepdims=True)
    var = jnp.mean(jnp.square(o - mu), axis=-1, keepdims=True)
    o = (o - mu) * lax.rsqrt(var + GN_EPS)
    o = o.transpose(0, 2, 1, 3).reshape(B, T, v_w)
    y = (jax.nn.silu(g.astype(jnp.float32)) * o).astype(h.dtype)
    return y @ w_out


def neighbourhood_attention(h, w_qkv, rpb, w_out):
    B, T, D = h.shape
    rows = T // GRID_W
    wr = min(NA_WIN_R, rows)
    qkv = (h @ w_qkv).reshape(B, rows, GRID_W, 3, NA_HEADS, NA_DH)
    qkv = jnp.transpose(qkv, (3, 0, 4, 1, 2, 5))
    q, k, v = qkv[0] * (NA_DH ** -0.5), qkv[1], qkv[2]

    col = jnp.arange(GRID_W)
    c_start = jnp.clip(col - NA_WIN_C // 2, 0, GRID_W - NA_WIN_C)
    col_mask = (col[None, :] >= c_start[:, None]) & (col[None, :] < c_start[:, None] + NA_WIN_C)
    dc_idx = jnp.clip(col[None, :] - col[:, None] + NA_WIN_C - 1, 0, 2 * NA_WIN_C - 2)

    def row_block(r):
        r_start = jnp.clip(r - wr // 2, 0, rows - wr)
        dr_idx = r_start + jnp.arange(wr) - r + NA_WIN_R - 1
        bias = rpb[:, dr_idx[:, None, None], dc_idx[None]]
        bias = jnp.transpose(bias, (0, 2, 1, 3)).astype(jnp.float32)
        q_r = lax.dynamic_index_in_dim(q, r, axis=2, keepdims=False)
        k_b = lax.dynamic_slice_in_dim(k, r_start, wr, axis=2)
        v_b = lax.dynamic_slice_in_dim(v, r_start, wr, axis=2)
        s = jnp.einsum('bhqd,bhrkd->bhqrk', q_r, k_b).astype(jnp.float32) + bias
        s = jnp.where(col_mask[:, None, :], s, -jnp.inf)
        p = jax.nn.softmax(s.reshape(B, NA_HEADS, GRID_W, wr * GRID_W), axis=-1).reshape(s.shape)
        return jnp.einsum('bhqrk,bhrkd->bhqd', p.astype(v_b.dtype), v_b)

    o = lax.map(row_block, jnp.arange(rows))
    o = jnp.transpose(o, (1, 0, 3, 2, 4)).reshape(B, T, NA_HEADS * NA_DH)
    return o @ w_out


def mla(h, w_down, q_norm, kv_norm, w_uq, w_ukv, w_out, pos):
    B, T, _ = h.shape
    H = MLA_HEADS
    down = h @ w_down
    c_q = rmsnorm(down[..., :MLA_Q_RANK], q_norm)
    c_kv = rmsnorm(down[..., MLA_Q_RANK:MLA_Q_RANK + MLA_KV_RANK], kv_norm)
    k_rope = down[..., MLA_Q_RANK + MLA_KV_RANK:]
    q = (c_q @ w_uq).reshape(B, T, H, MLA_NOPE + MLA_ROPE).transpose(0, 2, 1, 3)
    q = jnp.concatenate([q[..., :MLA_NOPE], rope(q[..., MLA_NOPE:], pos)], axis=-1)
    q = q * ((MLA_NOPE + MLA_ROPE) ** -0.5)
    kv = (c_kv @ w_ukv).reshape(B, T, H, MLA_NOPE + MLA_V).transpose(0, 2, 1, 3)
    k_rope = rope(k_rope[:, None], pos)
    k = jnp.concatenate([kv[..., :MLA_NOPE], jnp.broadcast_to(k_rope, (B, H, T, MLA_ROPE))], axis=-1)
    v = kv[..., MLA_NOPE:]
    nb = T // MLA_QBLOCK
    q_blocks = jnp.moveaxis(q.reshape(B, H, nb, MLA_QBLOCK, MLA_NOPE + MLA_ROPE), 2, 0)

    def attend(q_blk):
        s = jnp.einsum('bhqd,bhkd->bhqk', q_blk, k).astype(jnp.float32)
        p = jax.nn.softmax(s, axis=-1)
        return jnp.einsum('bhqk,bhkd->bhqd', p.astype(v.dtype), v)

    o = lax.map(attend, q_blocks)
    o = jnp.moveaxis(o, 0, 2).reshape(B, H, T, MLA_V).transpose(0, 2, 1, 3).reshape(B, T, H * MLA_V)
    return o @ w_out


def conv_ffn(h, w_up, conv_w, conv_b, w_down):
    u = h @ w_up
    u = lax.conv_general_dilated(
        u, conv_w[:, None, :], window_strides=(1,), padding=((CONV_W // 2, CONV_W // 2),),
        dimension_numbers=('NWC', 'WIO', 'NWC'), feature_group_count=2 * D_FF) + conv_b
    g, val = u[..., :D_FF], u[..., D_FF:]
    return (jax.nn.silu(g) * val) @ w_down


def setup_inputs(seed: int = 0) -> dict:
    key = jax.random.key(seed)
    keys = iter(jax.random.split(key, 64))
    f32 = jnp.float32

    def dense(shape, fan_in):
        return jax.random.normal(next(keys), shape, f32) * (fan_in ** -0.5)

    def gain(n):
        return 1.0 + 0.02 * jax.random.normal(next(keys), (n,), f32)

    def small(shape, s):
        return s * jax.random.normal(next(keys), shape, f32)

    def decay_logits():
        a = 5.0 + jnp.arange(RET_HEADS, dtype=f32)
        return jnp.log(jnp.exp2(a) - 1.0) + small((RET_HEADS,), 0.05)

    ret_in = 2 * RET_HEADS * RET_DK + 2 * RET_HEADS * RET_DV
    p = {}
    p["x"] = jax.random.normal(next(keys), (BATCH, SEQ, D_MODEL), f32)

    def add_ffn(pre):
        p[pre + "ffn_norm"] = gain(D_MODEL)
        p[pre + "ffn_w_up"] = dense((D_MODEL, 2 * D_FF), D_MODEL)
        p[pre + "ffn_conv_w"] = dense((CONV_W, 2 * D_FF), CONV_W)
        p[pre + "ffn_conv_b"] = small((2 * D_FF,), 0.01)
        p[pre + "ffn_w_down"] = dense((D_FF, D_MODEL), D_FF)

    def add_ret(pre):
        p[pre + "attn_norm"] = gain(D_MODEL)
        p[pre + "ret_w_in"] = dense((D_MODEL, ret_in), D_MODEL)
        p[pre + "ret_decay_fwd"] = decay_logits()
        p[pre + "ret_decay_bwd"] = decay_logits()
        p[pre + "ret_w_out"] = dense((RET_HEADS * RET_DV, D_MODEL), RET_HEADS * RET_DV)

    add_ret("l0_")
    add_ffn("l0_")
    p["l1_attn_norm"] = gain(D_MODEL)
    p["l1_na_w_qkv"] = dense((D_MODEL, 3 * NA_HEADS * NA_DH), D_MODEL)
    p["l1_na_rpb"] = small((NA_HEADS, 2 * NA_WIN_R - 1, 2 * NA_WIN_C - 1), 0.1)
    p["l1_na_w_out"] = dense((NA_HEADS * NA_DH, D_MODEL), NA_HEADS * NA_DH)
    add_ffn("l1_")
    p["l2_attn_norm"] = gain(D_MODEL)
    p["l2_mla_w_down"] = dense((D_MODEL, MLA_Q_RANK + MLA_KV_RANK + MLA_ROPE), D_MODEL)
    p["l2_mla_q_norm"] = gain(MLA_Q_RANK)
    p["l2_mla_kv_norm"] = gain(MLA_KV_RANK)
    p["l2_mla_w_uq"] = dense((MLA_Q_RANK, MLA_HEADS * (MLA_NOPE + MLA_ROPE)), MLA_Q_RANK)
    p["l2_mla_w_ukv"] = dense((MLA_KV_RANK, MLA_HEADS * (MLA_NOPE + MLA_V)), MLA_KV_RANK)
    p["l2_mla_w_out"] = dense((MLA_HEADS * MLA_V, D_MODEL), MLA_HEADS * MLA_V)
    add_ffn("l2_")
    add_ret("l3_")
    add_ffn("l3_")
    p["final_norm"] = gain(D_MODEL)
    return p


def reference(x,
              l0_attn_norm, l0_ret_w_in, l0_ret_decay_fwd, l0_ret_decay_bwd, l0_ret_w_out,
              l0_ffn_norm, l0_ffn_w_up, l0_ffn_conv_w, l0_ffn_conv_b, l0_ffn_w_down,
              l1_attn_norm, l1_na_w_qkv, l1_na_rpb, l1_na_w_out,
              l1_ffn_norm, l1_ffn_w_up, l1_ffn_conv_w, l1_ffn_conv_b, l1_ffn_w_down,
              l2_attn_norm, l2_mla_w_down, l2_mla_q_norm, l2_mla_kv_norm, l2_mla_w_uq, l2_mla_w_ukv, l2_mla_w_out,
              l2_ffn_norm, l2_ffn_w_up, l2_ffn_conv_w, l2_ffn_conv_b, l2_ffn_w_down,
              l3_attn_norm, l3_ret_w_in, l3_ret_decay_fwd, l3_ret_decay_bwd, l3_ret_w_out,
              l3_ffn_norm, l3_ffn_w_up, l3_ffn_conv_w, l3_ffn_conv_b, l3_ffn_w_down,
              final_norm):
    T = x.shape[1]
    pos = jnp.arange(T, dtype=jnp.float32)
    attn_norms = [l0_attn_norm, l1_attn_norm, l2_attn_norm, l3_attn_norm]
    mixer_params = [
        (l0_ret_w_in, l0_ret_decay_fwd, l0_ret_decay_bwd, l0_ret_w_out),
        (l1_na_w_qkv, l1_na_rpb, l1_na_w_out),
        (l2_mla_w_down, l2_mla_q_norm, l2_mla_kv_norm, l2_mla_w_uq, l2_mla_w_ukv, l2_mla_w_out),
        (l3_ret_w_in, l3_ret_decay_fwd, l3_ret_decay_bwd, l3_ret_w_out),
    ]
    ffn_norms = [l0_ffn_norm, l1_ffn_norm, l2_ffn_norm, l3_ffn_norm]
    ffn_params = [
        (l0_ffn_w_up, l0_ffn_conv_w, l0_ffn_conv_b, l0_ffn_w_down),
        (l1_ffn_w_up, l1_ffn_conv_w, l1_ffn_conv_b, l1_ffn_w_down),
        (l2_ffn_w_up, l2_ffn_conv_w, l2_ffn_conv_b, l2_ffn_w_down),
        (l3_ffn_w_up, l3_ffn_conv_w, l3_ffn_conv_b, l3_ffn_w_down),
    ]
    h = x
    for i in range(DEPTH):
        a = rmsnorm(h, attn_norms[i])
        kind = i % N_MIXERS
        if kind == 0:
            mix = retention_mixer(a, *mixer_params[i], pos)
        elif kind == 1:
            mix = neighbourhood_attention(a, *mixer_params[i])
        else:
            mix = mla(a, *mixer_params[i], pos)
        h = h + mix.astype(h.dtype)
        h = h + conv_ffn(rmsnorm(h, ffn_norms[i]), *ffn_params[i]).astype(h.dtype)
    return rmsnorm(h, final_norm)
```

```python
import functools

import jax
import jax.numpy as jnp
from jax import lax
from jax.experimental import pallas as pl
from jax.experimental.pallas import tpu as pltpu

F32 = jnp.float32
BF16 = jnp.bfloat16

GRID_W = 64
ROPE_BASE = 10000.0
NORM_EPS = 1e-6
GN_EPS = 1e-5

RET_HEADS = 8
RET_DK = 256
RET_DV = 512
RET_CHUNK = 256

NA_HEADS = 16
NA_DH = 128
NA_WIN_R = 8
NA_WIN_C = 16
NA_QROWS = 4
NA_KROWS = NA_QROWS + NA_WIN_R

MLA_HEADS = 16
MLA_Q_RANK = 512
MLA_KV_RANK = 512
MLA_NOPE = 128
MLA_ROPE = 64
MLA_V = 128
MLA_QK_PAD = 256
MLA_TQ = 512
MLA_TK = 1024

CONV_W = 3
MASK_NEG = -1e30

VMEM_LIMIT = 56 * 1024 * 1024


def _cparams(*sem):
    return pltpu.CompilerParams(dimension_semantics=sem, vmem_limit_bytes=VMEM_LIMIT)


def _dot(a, b):
    return jnp.dot(a, b, preferred_element_type=F32)


def _dot_nt(a, b):
    return lax.dot_general(a, b, (((1,), (1,)), ((), ())), preferred_element_type=F32)


def _dot_tn(a, b):
    return lax.dot_general(a, b, (((0,), (0,)), ((), ())), preferred_element_type=F32)


def _rms(x, g):
    return x * lax.rsqrt(jnp.mean(x * x, axis=-1, keepdims=True) + NORM_EPS) * g


def _pick(n, pref):
    if n <= pref:
        return n
    t = pref
    while n % t:
        t //= 2
    return t


def _norm_matmul_body(x_ref, g_ref, w_ref, cs_ref, o_ref, xn_ref):
    @pl.when(pl.program_id(1) == 0)
    def _():
        xn_ref[...] = _rms(x_ref[...], g_ref[...]).astype(BF16)

    o_ref[...] = (_dot(xn_ref[...], w_ref[...]) * cs_ref[...]).astype(o_ref.dtype)


def norm_matmul(x, g, w, col_scale=None, tm=1024, tn=1024):
    M, K = x.shape
    N = w.shape[1]
    tm, tn = _pick(M, tm), _pick(N, tn)
    if col_scale is None:
        col_scale = jnp.ones((N,), F32)
    return pl.pallas_call(
        _norm_matmul_body,
        grid=(M // tm, N // tn),
        in_specs=[
            pl.BlockSpec((tm, K), lambda i, j: (i, 0)),
            pl.BlockSpec((1, K), lambda i, j: (0, 0)),
            pl.BlockSpec((K, tn), lambda i, j: (0, j)),
            pl.BlockSpec((1, tn), lambda i, j: (0, j)),
        ],
        out_specs=pl.BlockSpec((tm, tn), lambda i, j: (i, j)),
        out_shape=jax.ShapeDtypeStruct((M, N), BF16),
        scratch_shapes=[pltpu.VMEM((tm, K), BF16)],
        compiler_params=_cparams("parallel", "arbitrary"),
        name="norm_matmul",
    )(x, g.reshape(1, K), w, col_scale.reshape(1, N))


def _matmul_res_body(a_ref, w_ref, r_ref, g_ref, o_ref, *, final_norm):
    k = pl.program_id(1)

    @pl.when(k == 0)
    def _():
        o_ref[...] = r_ref[...]

    o_ref[...] += _dot(a_ref[...], w_ref[...])

    if final_norm:
        @pl.when(k == pl.num_programs(1) - 1)
        def _():
            o_ref[...] = _rms(o_ref[...], g_ref[...])


def matmul_residual(a, w, r, final_gain=None, tm=1024, tk=1024):
    M, K = a.shape
    N = w.shape[1]
    tm = _pick(M, tm)
    if K % tk:
        tk = K // 4 if (K // 4) % 128 == 0 else K
    g = jnp.ones((N,), F32) if final_gain is None else final_gain
    return pl.pallas_call(
        functools.partial(_matmul_res_body, final_norm=final_gain is not None),
        grid=(M // tm, K // tk),
        in_specs=[
            pl.BlockSpec((tm, tk), lambda i, k: (i, k)),
            pl.BlockSpec((tk, N), lambda i, k: (k, 0)),
            pl.BlockSpec((tm, N), lambda i, k: (i, 0)),
            pl.BlockSpec((1, N), lambda i, k: (0, 0)),
        ],
        out_specs=pl.BlockSpec((tm, N), lambda i, k: (i, 0)),
        out_shape=jax.ShapeDtypeStruct((M, N), F32),
        compiler_params=_cparams("parallel", "arbitrary"),
        name="matmul_residual",
    )(a, w, r, g.reshape(1, N))


def _matmul_body(a_ref, w_ref, o_ref):
    o_ref[...] = _dot(a_ref[...], w_ref[...]).astype(o_ref.dtype)


def matmul(a, w, tm=1024, tn=1024):
    M, K = a.shape
    N = w.shape[1]
    tm, tn = _pick(M, tm), _pick(N, tn)
    return pl.pallas_call(
        _matmul_body,
        grid=(M // tm, N // tn),
        in_specs=[
            pl.BlockSpec((tm, K), lambda i, j: (i, 0)),
            pl.BlockSpec((K, tn), lambda i, j: (0, j)),
        ],
        out_specs=pl.BlockSpec((tm, tn), lambda i, j: (i, j)),
        out_shape=jax.ShapeDtypeStruct((M, N), BF16),
        compiler_params=_cparams("parallel", "parallel"),
        name="matmul",
    )(a, w)


def _rope_tables(T, d):
    half = d // 2
    inv = ROPE_BASE ** (-jnp.arange(half, dtype=F32) * 2.0 / d)
    ang = jnp.arange(T, dtype=F32)[:, None] * inv[None, :]
    return jnp.cos(ang), jnp.sin(ang)


def _retention_body(lgf_ref, lgb_ref, q_ref, k_ref, v_ref, g_ref, cos_ref, sin_ref, y_ref,
                    state_ref, of_ref, *, n_chunks):
    h = pl.program_id(1)
    s = pl.program_id(2)
    C = q_ref.shape[0]
    half = q_ref.shape[1] // 2

    cos = cos_ref[...]
    sin = sin_ref[...]

    def rope(x):
        x1, x2 = x[:, :half], x[:, half:]
        return jnp.concatenate([x1 * cos - x2 * sin, x1 * sin + x2 * cos], axis=1)

    q = rope(q_ref[...].astype(F32))
    k = rope(k_ref[...].astype(F32)) * (RET_DK ** -0.5)
    v = v_ref[...]
    qb = q.astype(BF16)
    kb = k.astype(BF16)

    row = lax.broadcasted_iota(jnp.int32, (C, C), 0)
    col = lax.broadcasted_iota(jnp.int32, (C, C), 1)
    idx = lax.broadcasted_iota(jnp.int32, (C, 1), 0).astype(F32)

    def direction(lg, diff, mask, q_pow, k_pow):
        decay = jnp.where(mask, jnp.exp(jnp.where(mask, diff, 0).astype(F32) * lg), 0.0)
        scores = (_dot_nt(qb, kb) * decay).astype(BF16)
        out = _dot(scores, v) + _dot((q * jnp.exp(q_pow * lg)).astype(BF16),
                                     state_ref[...].astype(BF16))
        state_ref[...] = (state_ref[...] * jnp.exp(C * lg)
                          + _dot_tn((k * jnp.exp(k_pow * lg)).astype(BF16), v))
        return out

    @pl.when(s % n_chunks == 0)
    def _():
        state_ref[...] = jnp.zeros_like(state_ref)

    @pl.when(s < n_chunks)
    def _():
        out = direction(lgf_ref[h], row - col, row >= col, idx + 1.0, C - 1.0 - idx)
        of_ref[pl.ds(pl.multiple_of(s * C, C), C), :] = out

    @pl.when(s >= n_chunks)
    def _():
        c = 2 * n_chunks - 1 - s
        out = direction(lgb_ref[h], col - row, col > row, C - idx, idx)
        o = out + of_ref[pl.ds(pl.multiple_of(c * C, C), C), :]
        mu = jnp.mean(o, axis=-1, keepdims=True)
        d = o - mu
        var = jnp.mean(d * d, axis=-1, keepdims=True)
        o = d * lax.rsqrt(var + GN_EPS)
        g = g_ref[...].astype(F32)
        y_ref[...] = (g * jax.nn.sigmoid(g) * o).astype(y_ref.dtype)


def retention_core(proj, lg_f, lg_b, B, T):
    H, dk, dv = RET_HEADS, RET_DK, RET_DV
    C = min(RET_CHUNK, T)
    N = T // C
    cos, sin = _rope_tables(T, dk)
    v_blk0 = 2 * H * dk // dv
    g_blk0 = v_blk0 + H

    def chunk(s):
        return jnp.where(s < N, s, 2 * N - 1 - s)

    def gate_chunk(s):
        return jnp.where(s < N, N - 1, 2 * N - 1 - s)

    smem = pl.BlockSpec(memory_space=pltpu.SMEM)
    return pl.pallas_call(
        functools.partial(_retention_body, n_chunks=N),
        grid=(B, H, 2 * N),
        in_specs=[
            smem, smem,
            pl.BlockSpec((C, dk), lambda b, h, s: (b * N + chunk(s), h)),
            pl.BlockSpec((C, dk), lambda b, h, s: (b * N + chunk(s), H + h)),
            pl.BlockSpec((C, dv), lambda b, h, s: (b * N + chunk(s), v_blk0 + h)),
            pl.BlockSpec((C, dv), lambda b, h, s: (b * N + gate_chunk(s), g_blk0 + h)),
            pl.BlockSpec((C, dk // 2), lambda b, h, s: (chunk(s), 0)),
            pl.BlockSpec((C, dk // 2), lambda b, h, s: (chunk(s), 0)),
        ],
        out_specs=pl.BlockSpec((C, dv), lambda b, h, s: (b * N + gate_chunk(s), h)),
        out_shape=jax.ShapeDtypeStruct((B * T, H * dv), BF16),
        scratch_shapes=[pltpu.VMEM((dk, dv), F32), pltpu.VMEM((T, dv), F32)],
        compiler_params=_cparams("parallel", "parallel", "arbitrary"),
        name="retention",
    )(lg_f, lg_b, proj, proj, proj, proj, cos, sin)


def retention_mixer(h2d, gain, w_in, decay_fwd, decay_bwd, w_out, B, T):
    proj = norm_matmul(h2d, gain, w_in.astype(BF16))
    lg_f = jax.nn.log_sigmoid(decay_fwd.astype(F32))
    lg_b = jax.nn.log_sigmoid(decay_bwd.astype(F32))
    y = retention_core(proj, lg_f, lg_b, B, T)
    return matmul_residual(y, w_out.astype(BF16), h2d)


def _na_bias_tables(rpb, rows):
    W, wr, R, KR = GRID_W, NA_WIN_R, NA_QROWS, NA_KROWS
    col = jnp.arange(W)
    c_start = jnp.clip(col - NA_WIN_C // 2, 0, W - NA_WIN_C)
    col_ok = (col[None, :] >= c_start[:, None]) & (col[None, :] < c_start[:, None] + NA_WIN_C)
    dc_idx = jnp.clip(col[None, :] - col[:, None] + NA_WIN_C - 1, 0, 2 * NA_WIN_C - 2)
    a = jnp.arange(R)[:, None]
    bb = jnp.arange(KR)[None, :]
    tables = []
    for r0, ks in ((0, 0), (R, 0), (rows - R, rows - KR)):
        rq = r0 + a
        rk = ks + bb
        r_start = jnp.clip(rq - wr // 2, 0, rows - wr)
        row_ok = (rk >= r_start) & (rk < r_start + wr)
        dr_idx = jnp.clip(rk - rq + NA_WIN_R - 1, 0, 2 * NA_WIN_R - 2)
        bias = rpb[:, dr_idx[:, None, :, None], dc_idx[None, :, None, :]]
        ok = row_ok[:, None, :, None] & col_ok[None, :, None, :]
        bias = jnp.where(ok[None], bias.astype(F32), MASK_NEG)
        tables.append(bias.reshape(rpb.shape[0], R * W, KR * W))
    return jnp.stack(tables, axis=1)


def _na_body(q_ref, k_ref, v_ref, bias_ref, o_ref, *, rows):
    W, R, KR = GRID_W, NA_QROWS, NA_KROWS
    nb = rows // R
    tq, tk = R * W, KR * W

    def block(i, carry):
        ks = jnp.clip(R * i - R, 0, rows - KR)
        case = jnp.where(i == 0, 0, jnp.where(i == nb - 1, 2, 1))
        q0 = pl.multiple_of(i * tq, tq)
        k0 = pl.multiple_of(ks * W, R * W)
        q = q_ref[pl.ds(q0, tq), :]
        k = k_ref[pl.ds(k0, tk), :]
        v = v_ref[pl.ds(k0, tk), :]
        s = _dot_nt(q, k) + bias_ref[0, case]
        m = jnp.max(s, axis=-1, keepdims=True)
        p = jnp.exp(s - m)
        p = p / jnp.sum(p, axis=-1, keepdims=True)
        o_ref[pl.ds(q0, tq), :] = _dot(p.astype(BF16), v).astype(o_ref.dtype)
        return carry

    lax.fori_loop(0, nb, block, 0)


def na_core(qkv, bias_tables, B, T):
    H, dh = NA_HEADS, NA_DH
    rows = T // GRID_W
    tq, tk = NA_QROWS * GRID_W, NA_KROWS * GRID_W
    return pl.pallas_call(
        functools.partial(_na_body, rows=rows),
        grid=(H, B),
        in_specs=[
            pl.BlockSpec((T, dh), lambda h, b: (b, h)),
            pl.BlockSpec((T, dh), lambda h, b: (b, H + h)),
            pl.BlockSpec((T, dh), lambda h, b: (b, 2 * H + h)),
            pl.BlockSpec((1, 3, tq, tk), lambda h, b: (h, 0, 0, 0)),
        ],
        out_specs=pl.BlockSpec((T, dh), lambda h, b: (b, h)),
        out_shape=jax.ShapeDtypeStruct((B * T, H * dh), BF16),
        compiler_params=_cparams("parallel", "parallel"),
        name="neighbourhood_attention",
    )(qkv, qkv, qkv, bias_tables)


def na_mixer(h2d, gain, w_qkv, rpb, w_out, B, T):
    H, dh = NA_HEADS, NA_DH
    rows = T // GRID_W
    assert rows % NA_QROWS == 0 and rows >= NA_KROWS
    col_scale = jnp.concatenate([jnp.full((H * dh,), dh ** -0.5, F32), jnp.ones((2 * H * dh,), F32)])
    qkv = norm_matmul(h2d, gain, w_qkv.astype(BF16), col_scale)
    o = na_core(qkv, _na_bias_tables(rpb, rows), B, T)
    return matmul_residual(o, w_out.astype(BF16), h2d)


def _mla_down_body(x_ref, g_ref, wq_ref, wkv_ref, wr_ref, wrs_ref, qn_ref, kvn_ref,
                   cos_ref, sin_ref, cq_ref, ckv_ref, kr_ref):
    xn = _rms(x_ref[...], g_ref[...]).astype(BF16)
    cq_ref[...] = _rms(_dot(xn, wq_ref[...]), qn_ref[...]).astype(cq_ref.dtype)
    ckv_ref[...] = _rms(_dot(xn, wkv_ref[...]), kvn_ref[...]).astype(ckv_ref.dtype)
    kr = _dot(xn, wr_ref[...]) * cos_ref[...] + _dot(xn, wrs_ref[...]) * sin_ref[...]
    kr_ref[...] = kr.astype(kr_ref.dtype)


def _swap_halves(w, group):
    K, N = w.shape
    w = w.reshape(K, N // group, 2, group // 2)
    return w[:, :, ::-1, :].reshape(K, N)


def mla_down(h2d, gain, w_down, q_norm, kv_norm, T, tm=1024):
    M, D = h2d.shape
    tm = _pick(T, tm)
    nT = T // tm
    w = w_down.astype(BF16)
    qr, kvr, rp = MLA_Q_RANK, MLA_KV_RANK, MLA_ROPE
    w_q, w_kv, w_r = w[:, :qr], w[:, qr:qr + kvr], w[:, qr + kvr:]
    pad = jnp.zeros((D, 128 - rp), BF16)
    w_rs = jnp.concatenate([_swap_halves(w_r, rp), pad], axis=1)
    w_r = jnp.concatenate([w_r, pad], axis=1)
    cos, sin = _rope_tables(T, rp)
    zpad = jnp.zeros((T, 128 - rp), F32)
    cos_t = jnp.concatenate([cos, cos, zpad], axis=1)
    sin_t = jnp.concatenate([-sin, sin, zpad], axis=1)
    full = lambda shape: pl.BlockSpec(shape, lambda i: (0, 0))
    return pl.pallas_call(
        _mla_down_body,
        grid=(M // tm,),
        in_specs=[
            pl.BlockSpec((tm, D), lambda i: (i, 0)),
            full((1, D)), full((D, qr)), full((D, kvr)), full((D, 128)), full((D, 128)),
            full((1, qr)), full((1, kvr)),
            pl.BlockSpec((tm, 128), lambda i: (i % nT, 0)),
            pl.BlockSpec((tm, 128), lambda i: (i % nT, 0)),
        ],
        out_specs=[
            pl.BlockSpec((tm, qr), lambda i: (i, 0)),
            pl.BlockSpec((tm, kvr), lambda i: (i, 0)),
            pl.BlockSpec((tm, 128), lambda i: (i, 0)),
        ],
        out_shape=[
            jax.ShapeDtypeStruct((M, qr), BF16),
            jax.ShapeDtypeStruct((M, kvr), BF16),
            jax.ShapeDtypeStruct((M, 128), BF16),
        ],
        compiler_params=_cparams("parallel"),
        name="mla_down",
    )(h2d, gain.reshape(1, D), w_q, w_kv, w_r, w_rs, q_norm.reshape(1, qr), kv_norm.reshape(1, kvr),
      cos_t, sin_t)


def _mla_q_body(c_ref, w_ref, ws_ref, cos_ref, sin_ref, o_ref):
    c = c_ref[...]
    P = MLA_QK_PAD
    scale = (MLA_NOPE + MLA_ROPE) ** -0.5
    cos = cos_ref[...]
    sin = sin_ref[...]
    for j in range(o_ref.shape[1] // P):
        sl = slice(j * P, (j + 1) * P)
        x = _dot(c, w_ref[:, sl])
        xs = _dot(c, ws_ref[:, sl])
        o_ref[:, sl] = ((x * cos + xs * sin) * scale).astype(o_ref.dtype)


def mla_q(c_q, w_uq, T, tm=1024, heads_per_step=4):
    M, R = c_q.shape
    H, P, nope, rp = MLA_HEADS, MLA_QK_PAD, MLA_NOPE, MLA_ROPE
    tm = _pick(T, tm)
    nT = T // tm
    w = w_uq.astype(BF16).reshape(R, H, nope + rp)
    zeros = lambda n: jnp.zeros((R, H, n), BF16)
    w_rope = w[:, :, nope:]
    w_main = jnp.concatenate([w[:, :, :nope], w_rope, zeros(P - nope - rp)], axis=2).reshape(R, H * P)
    w_rope_sw = _swap_halves(w_rope.reshape(R, H * rp), rp).reshape(R, H, rp)
    w_swap = jnp.concatenate([zeros(nope), w_rope_sw, zeros(P - nope - rp)], axis=2).reshape(R, H * P)
    cos, sin = _rope_tables(T, rp)
    cos_t = jnp.concatenate([jnp.ones((T, nope), F32), cos, cos, jnp.zeros((T, P - nope - rp), F32)], axis=1)
    sin_t = jnp.concatenate([jnp.zeros((T, nope), F32), -sin, sin, jnp.zeros((T, P - nope - rp), F32)], axis=1)
    tn = heads_per_step * P
    return pl.pallas_call(
        _mla_q_body,
        grid=(M // tm, H * P // tn),
        in_specs=[
            pl.BlockSpec((tm, R), lambda i, j: (i, 0)),
            pl.BlockSpec((R, tn), lambda i, j: (0, j)),
            pl.BlockSpec((R, tn), lambda i, j: (0, j)),
            pl.BlockSpec((tm, P), lambda i, j: (i % nT, 0)),
            pl.BlockSpec((tm, P), lambda i, j: (i % nT, 0)),
        ],
        out_specs=pl.BlockSpec((tm, tn), lambda i, j: (i, j)),
        out_shape=jax.ShapeDtypeStruct((M, H * P), BF16),
        compiler_params=_cparams("parallel", "parallel"),
        name="mla_q",
    )(c_q, w_main, w_swap, cos_t, sin_t)


def _mla_attn_body(q_ref, kn_ref, kr_ref, v_ref, o_ref, kcat_ref, *, tk):
    T = kn_ref.shape[0]

    @pl.when(pl.program_id(2) == 0)
    def _():
        kcat_ref[:, :MLA_NOPE] = kn_ref[...]
        kcat_ref[:, MLA_NOPE:] = kr_ref[...]

    q = q_ref[...]
    tq = q.shape[0]

    def chunk(j, carry):
        m, l, acc = carry
        k0 = pl.multiple_of(j * tk, tk)
        s = _dot_nt(q, kcat_ref[pl.ds(k0, tk), :])
        m_new = jnp.maximum(m, jnp.max(s, axis=-1, keepdims=True))
        alpha = jnp.exp(m - m_new)
        p = jnp.exp(s - m_new)
        l = alpha * l + jnp.sum(p, axis=-1, keepdims=True)
        acc = alpha * acc + _dot(p.astype(BF16), v_ref[pl.ds(k0, tk), :])
        return m_new, l, acc

    init = (jnp.full((tq, 1), -jnp.inf, F32), jnp.zeros((tq, 1), F32), jnp.zeros((tq, MLA_V), F32))
    _, l, acc = lax.fori_loop(0, T // tk, chunk, init)
    o_ref[...] = (acc / l).astype(o_ref.dtype)


def mla_attention(q, kv, k_rope, B, T):
    H, P = MLA_HEADS, MLA_QK_PAD
    tq, tk = _pick(T, MLA_TQ), _pick(T, MLA_TK)
    nq = T // tq
    return pl.pallas_call(
        functools.partial(_mla_attn_body, tk=tk),
        grid=(B, H, nq),
        in_specs=[
            pl.BlockSpec((tq, P), lambda b, h, i: (b * nq + i, h)),
            pl.BlockSpec((T, MLA_NOPE), lambda b, h, i: (b, h)),
            pl.BlockSpec((T, 128), lambda b, h, i: (b, 0)),
            pl.BlockSpec((T, MLA_V), lambda b, h, i: (b, H + h)),
        ],
        out_specs=pl.BlockSpec((tq, MLA_V), lambda b, h, i: (b * nq + i, h)),
        out_shape=jax.ShapeDtypeStruct((B * T, H * MLA_V), BF16),
        scratch_shapes=[pltpu.VMEM((T, P), BF16)],
        compiler_params=_cparams("parallel", "parallel", "arbitrary"),
        name="mla_attention",
    )(q, kv, k_rope, kv)


def mla_mixer(h2d, gain, w_down, q_norm, kv_norm, w_uq, w_ukv, w_out, B, T):
    H = MLA_HEADS
    c_q, c_kv, k_rope = mla_down(h2d, gain, w_down, q_norm, kv_norm, T)
    q = mla_q(c_q, w_uq, T)
    w = w_ukv.astype(BF16).reshape(MLA_KV_RANK, H, MLA_NOPE + MLA_V)
    w_kv = jnp.concatenate([w[:, :, :MLA_NOPE].reshape(MLA_KV_RANK, H * MLA_NOPE),
                            w[:, :, MLA_NOPE:].reshape(MLA_KV_RANK, H * MLA_V)], axis=1)
    kv = matmul(c_kv, w_kv)
    o = mla_attention(q, kv, k_rope, B, T)
    return matmul_residual(o, w_out.astype(BF16), h2d)


def _conv_gate_body(ug_ref, uv_ref, wg_ref, wv_ref, bg_ref, bv_ref, o_ref, *, rc):
    T = ug_ref.shape[0]
    first = lax.broadcasted_iota(jnp.int32, (rc, 1), 0) == 0
    last = lax.broadcasted_iota(jnp.int32, (rc, 1), 0) == rc - 1

    def conv(u_ref, w_ref, b_ref, r0):
        x = u_ref[r0:r0 + rc, :].astype(F32)
        zero = jnp.zeros((1, x.shape[1]), F32)
        before = u_ref[r0 - 16:r0, :].astype(F32)[15:16, :] if r0 > 0 else zero
        after = u_ref[r0 + rc:r0 + rc + 16, :].astype(F32)[0:1, :] if r0 + rc < T else zero
        prev = jnp.where(first, before, pltpu.roll(x, 1, axis=0))
        nxt = jnp.where(last, after, pltpu.roll(x, rc - 1, axis=0))
        return prev * w_ref[0:1, :] + x * w_ref[1:2, :] + nxt * w_ref[2:3, :] + b_ref[...]

    for r0 in range(0, T, rc):
        g = conv(ug_ref, wg_ref, bg_ref, r0)
        val = conv(uv_ref, wv_ref, bv_ref, r0)
        o_ref[r0:r0 + rc, :] = (g * jax.nn.sigmoid(g) * val).astype(o_ref.dtype)


def conv_gate(u, conv_w, conv_b, B, T, tc=256, rc=512):
    F = u.shape[1] // 2
    tc = _pick(F, tc)
    rc = _pick(T, rc)
    nj = F // tc
    b2 = conv_b.reshape(1, 2 * F).astype(F32)
    w = conv_w.astype(F32)
    return pl.pallas_call(
        functools.partial(_conv_gate_body, rc=rc),
        grid=(B, nj),
        in_specs=[
            pl.BlockSpec((T, tc), lambda b, j: (b, j)),
            pl.BlockSpec((T, tc), lambda b, j: (b, nj + j)),
            pl.BlockSpec((CONV_W, tc), lambda b, j: (0, j)),
            pl.BlockSpec((CONV_W, tc), lambda b, j: (0, nj + j)),
            pl.BlockSpec((1, tc), lambda b, j: (0, j)),
            pl.BlockSpec((1, tc), lambda b, j: (0, nj + j)),
        ],
        out_specs=pl.BlockSpec((T, tc), lambda b, j: (b, j)),
        out_shape=jax.ShapeDtypeStruct((B * T, F), BF16),
        compiler_params=_cparams("parallel", "parallel"),
        name="conv_gate",
    )(u, u, w, w, b2, b2)


def conv_ffn(h2d, gain, w_up, conv_w, conv_b, w_down, B, T, final_gain=None):
    u = norm_matmul(h2d, gain, w_up.astype(BF16))
    a = conv_gate(u, conv_w, conv_b, B, T)
    return matmul_residual(a, w_down.astype(BF16), h2d, final_gain=final_gain)


def kernel(x, l0_attn_norm, l0_ret_w_in, l0_ret_decay_fwd, l0_ret_decay_bwd, l0_ret_w_out, l0_ffn_norm, l0_ffn_w_up, l0_ffn_conv_w, l0_ffn_conv_b, l0_ffn_w_down, l1_attn_norm, l1_na_w_qkv, l1_na_rpb, l1_na_w_out, l1_ffn_norm, l1_ffn_w_up, l1_ffn_conv_w, l1_ffn_conv_b, l1_ffn_w_down, l2_attn_norm, l2_mla_w_down, l2_mla_q_norm, l2_mla_kv_norm, l2_mla_w_uq, l2_mla_w_ukv, l2_mla_w_out, l2_ffn_norm, l2_ffn_w_up, l2_ffn_conv_w, l2_ffn_conv_b, l2_ffn_w_down, l3_attn_norm, l3_ret_w_in, l3_ret_decay_fwd, l3_ret_decay_bwd, l3_ret_w_out, l3_ffn_norm, l3_ffn_w_up, l3_ffn_conv_w, l3_ffn_conv_b, l3_ffn_w_down, final_norm):
    B, T, D = x.shape
    h = x.reshape(B * T, D)
    h = retention_mixer(h, l0_attn_norm, l0_ret_w_in, l0_ret_decay_fwd, l0_ret_decay_bwd, l0_ret_w_out, B, T)
    h = conv_ffn(h, l0_ffn_norm, l0_ffn_w_up, l0_ffn_conv_w, l0_ffn_conv_b, l0_ffn_w_down, B, T)
    h = na_mixer(h, l1_attn_norm, l1_na_w_qkv, l1_na_rpb, l1_na_w_out, B, T)
    h = conv_ffn(h, l1_ffn_norm, l1_ffn_w_up, l1_ffn_conv_w, l1_ffn_conv_b, l1_ffn_w_down, B, T)
    h = mla_mixer(h, l2_attn_norm, l2_mla_w_down, l2_mla_q_norm, l2_mla_kv_norm, l2_mla_w_uq, l2_mla_w_ukv, l2_mla_w_out, B, T)
    h = conv_ffn(h, l2_ffn_norm, l2_ffn_w_up, l2_ffn_conv_w, l2_ffn_conv_b, l2_ffn_w_down, B, T)
    h = retention_mixer(h, l3_attn_norm, l3_ret_w_in, l3_ret_decay_fwd, l3_ret_decay_bwd, l3_ret_w_out, B, T)
    h = conv_ffn(h, l3_ffn_norm, l3_ffn_w_up, l3_ffn_conv_w, l3_ffn_conv_b, l3_ffn_w_down, B, T,
                 final_gain=final_norm)
    return h.reshape(B, T, D)
```

```python
import functools

import jax
import jax.numpy as jnp
from jax import lax
from jax.experimental import pallas as pl
from jax.experimental.pallas import tpu as pltpu

F32 = jnp.float32
BF16 = jnp.bfloat16

GRID_W = 64
ROPE_BASE = 10000.0
NORM_EPS = 1e-6
GN_EPS = 1e-5

RET_HEADS = 8
RET_DK = 256
RET_DV = 512
RET_CHUNK = 256

NA_HEADS = 16
NA_DH = 128
NA_WIN_R = 8
NA_WIN_C = 16
NA_QROWS = 4
NA_KROWS = NA_QROWS + NA_WIN_R

MLA_HEADS = 16
MLA_Q_RANK = 512
MLA_KV_RANK = 512
MLA_NOPE = 128
MLA_ROPE = 64
MLA_V = 128
MLA_QK_PAD = 256
MLA_TQ = 512
MLA_TK = 512
LOG2_E = 1.4426950408889634

CONV_W = 3
MASK_NEG = -1e30

VMEM_LIMIT = 56 * 1024 * 1024


def _cparams(*sem):
    return pltpu.CompilerParams(dimension_semantics=sem, vmem_limit_bytes=VMEM_LIMIT)


def _dot(a, b):
    return jnp.dot(a, b, preferred_element_type=F32)


def _dot_nt(a, b):
    return lax.dot_general(a, b, (((1,), (1,)), ((), ())), preferred_element_type=F32)


def _dot_tn(a, b):
    return lax.dot_general(a, b, (((0,), (0,)), ((), ())), preferred_element_type=F32)


def _rms(x, g):
    return x * lax.rsqrt(jnp.mean(x * x, axis=-1, keepdims=True) + NORM_EPS) * g


def _pick(n, pref):
    if n <= pref:
        return n
    t = pref
    while n % t:
        t //= 2
    return t


def _norm_matmul_body(x_ref, g_ref, w_ref, cs_ref, o_ref, xn_ref):
    @pl.when(pl.program_id(1) == 0)
    def _():
        xn_ref[...] = _rms(x_ref[...], g_ref[...]).astype(BF16)

    o_ref[...] = (_dot(xn_ref[...], w_ref[...]) * cs_ref[...]).astype(o_ref.dtype)


def norm_matmul(x, g, w, col_scale=None, tm=1024, tn=1024):
    M, K = x.shape
    N = w.shape[1]
    tm, tn = _pick(M, tm), _pick(N, tn)
    if col_scale is None:
        col_scale = jnp.ones((N,), F32)
    return pl.pallas_call(
        _norm_matmul_body,
        grid=(M // tm, N // tn),
        in_specs=[
            pl.BlockSpec((tm, K), lambda i, j: (i, 0)),
            pl.BlockSpec((1, K), lambda i, j: (0, 0)),
            pl.BlockSpec((K, tn), lambda i, j: (0, j)),
            pl.BlockSpec((1, tn), lambda i, j: (0, j)),
        ],
        out_specs=pl.BlockSpec((tm, tn), lambda i, j: (i, j)),
        out_shape=jax.ShapeDtypeStruct((M, N), BF16),
        scratch_shapes=[pltpu.VMEM((tm, K), BF16)],
        compiler_params=_cparams("parallel", "arbitrary"),
        name="norm_matmul",
    )(x, g.reshape(1, K), w, col_scale.reshape(1, N))


def _matmul_res_body(a_ref, w_ref, r_ref, g_ref, o_ref, *, final_norm):
    k = pl.program_id(1)

    @pl.when(k == 0)
    def _():
        o_ref[...] = r_ref[...]

    o_ref[...] += _dot(a_ref[...], w_ref[...])

    if final_norm:
        @pl.when(k == pl.num_programs(1) - 1)
        def _():
            o_ref[...] = _rms(o_ref[...], g_ref[...])


def matmul_residual(a, w, r, final_gain=None, tm=1024, tk=1024):
    M, K = a.shape
    N = w.shape[1]
    tm = _pick(M, tm)
    if K % tk:
        tk = K // 4 if (K // 4) % 128 == 0 else K
    g = jnp.ones((N,), F32) if final_gain is None else final_gain
    return pl.pallas_call(
        functools.partial(_matmul_res_body, final_norm=final_gain is not None),
        grid=(M // tm, K // tk),
        in_specs=[
            pl.BlockSpec((tm, tk), lambda i, k: (i, k)),
            pl.BlockSpec((tk, N), lambda i, k: (k, 0)),
            pl.BlockSpec((tm, N), lambda i, k: (i, 0)),
            pl.BlockSpec((1, N), lambda i, k: (0, 0)),
        ],
        out_specs=pl.BlockSpec((tm, N), lambda i, k: (i, 0)),
        out_shape=jax.ShapeDtypeStruct((M, N), F32),
        compiler_params=_cparams("parallel", "arbitrary"),
        name="matmul_residual",
    )(a, w, r, g.reshape(1, N))


def _matmul_body(a_ref, w_ref, o_ref):
    o_ref[...] = _dot(a_ref[...], w_ref[...]).astype(o_ref.dtype)


def matmul(a, w, tm=1024, tn=1024):
    M, K = a.shape
    N = w.shape[1]
    tm, tn = _pick(M, tm), _pick(N, tn)
    return pl.pallas_call(
        _matmul_body,
        grid=(M // tm, N // tn),
        in_specs=[
            pl.BlockSpec((tm, K), lambda i, j: (i, 0)),
            pl.BlockSpec((K, tn), lambda i, j: (0, j)),
        ],
        out_specs=pl.BlockSpec((tm, tn), lambda i, j: (i, j)),
        out_shape=jax.ShapeDtypeStruct((M, N), BF16),
        compiler_params=_cparams("parallel", "parallel"),
        name="matmul",
    )(a, w)


def _rope_tables(T, d):
    half = d // 2
    inv = ROPE_BASE ** (-jnp.arange(half, dtype=F32) * 2.0 / d)
    ang = jnp.arange(T, dtype=F32)[:, None] * inv[None, :]
    return jnp.cos(ang), jnp.sin(ang)


def _ret_in_body(x_ref, g_ref, w_ref, cos_ref, sin_ref, o_ref, xn_ref, *, n_q_tiles):
    j = pl.program_id(1)

    @pl.when(j == 0)
    def _():
        xn_ref[...] = _rms(x_ref[...], g_ref[...]).astype(BF16)

    acc = _dot(xn_ref[...], w_ref[...])

    @pl.when(j < 2 * n_q_tiles)
    def _():
        cos = cos_ref[...]
        sin = sin_ref[...]
        scale = jnp.where(j >= n_q_tiles, RET_DK ** -0.5, 1.0).astype(F32)
        half = RET_DK // 2
        for c0 in range(0, acc.shape[1], RET_DK):
            x1 = acc[:, c0:c0 + half]
            x2 = acc[:, c0 + half:c0 + RET_DK]
            o_ref[:, c0:c0 + half] = ((x1 * cos - x2 * sin) * scale).astype(o_ref.dtype)
            o_ref[:, c0 + half:c0 + RET_DK] = ((x1 * sin + x2 * cos) * scale).astype(o_ref.dtype)

    @pl.when(j >= 2 * n_q_tiles)
    def _():
        o_ref[...] = acc.astype(o_ref.dtype)


def ret_in_proj(x, g, w, T, tm=1024, tn=1024):
    M, K = x.shape
    N = w.shape[1]
    tm, tn = _pick(T, tm), _pick(N, tn)
    nT = T // tm
    qk_w = RET_HEADS * RET_DK
    assert tn % RET_DK == 0 and qk_w % tn == 0
    cos, sin = _rope_tables(T, RET_DK)
    return pl.pallas_call(
        functools.partial(_ret_in_body, n_q_tiles=qk_w // tn),
        grid=(M // tm, N // tn),
        in_specs=[
            pl.BlockSpec((tm, K), lambda i, j: (i, 0)),
            pl.BlockSpec((1, K), lambda i, j: (0, 0)),
            pl.BlockSpec((K, tn), lambda i, j: (0, j)),
            pl.BlockSpec((tm, RET_DK // 2), lambda i, j: (i % nT, 0)),
            pl.BlockSpec((tm, RET_DK // 2), lambda i, j: (i % nT, 0)),
        ],
        out_specs=pl.BlockSpec((tm, tn), lambda i, j: (i, j)),
        out_shape=jax.ShapeDtypeStruct((M, N), BF16),
        scratch_shapes=[pltpu.VMEM((tm, K), BF16)],
        compiler_params=_cparams("parallel", "arbitrary"),
        name="ret_in_proj",
    )(x, g.reshape(1, K), w, cos, sin)


def _retention_body(lgf_ref, lgb_ref, q_ref, k_ref, v_ref, g_ref, y_ref,
                    decay_ref, sf_all_ref, sf_ref, sb_ref, *, C):
    h = pl.program_id(1)
    T = q_ref.shape[0]
    N = T // C
    lg_f = lgf_ref[h]
    lg_b = lgb_ref[h]

    idx = lax.broadcasted_iota(jnp.int32, (C, 1), 0).astype(F32)
    kd_f = jnp.exp((C - 1.0 - idx) * lg_f)
    kd_b = jnp.exp(idx * lg_b)
    qd_f = jnp.exp((idx + 1.0) * lg_f)
    qd_b = jnp.exp((C - idx) * lg_b)
    chunk_len = jnp.full((1, 1), C, F32)
    cd_f = jnp.exp(chunk_len * lg_f)
    cd_b = jnp.exp(chunk_len * lg_b)

    row = lax.broadcasted_iota(jnp.int32, (C, C), 0)
    col = lax.broadcasted_iota(jnp.int32, (C, C), 1)
    decay_ref[...] = jnp.exp(jnp.where(row >= col, (row - col).astype(F32) * lg_f,
                                       (col - row).astype(F32) * lg_b))

    def rows(n):
        return pl.ds(pl.multiple_of(n * C, C), C)

    sf_ref[...] = jnp.zeros_like(sf_ref)
    sb_ref[...] = jnp.zeros_like(sb_ref)

    def forward_state(n, carry):
        k = k_ref[rows(n), :].astype(F32)
        sf_all_ref[n] = sf_ref[...].astype(BF16)
        sf_ref[...] = sf_ref[...] * cd_f + _dot_tn((k * kd_f).astype(BF16), v_ref[rows(n), :])
        return carry

    lax.fori_loop(0, N, forward_state, 0)

    def output_chunk(i, carry):
        n = N - 1 - i
        q = q_ref[rows(n), :]
        k = k_ref[rows(n), :]
        v = v_ref[rows(n), :]
        qf = q.astype(F32)
        scores = (_dot_nt(q, k) * decay_ref[...]).astype(BF16)
        o = (_dot(scores, v)
             + _dot((qf * qd_f).astype(BF16), sf_all_ref[n])
             + _dot((qf * qd_b).astype(BF16), sb_ref[...].astype(BF16)))
        mu = jnp.mean(o, axis=-1, keepdims=True)
        d = o - mu
        var = jnp.mean(d * d, axis=-1, keepdims=True)
        o = d * lax.rsqrt(var + GN_EPS)
        g = g_ref[rows(n), :].astype(F32)
        y_ref[rows(n), :] = (g * jax.nn.sigmoid(g) * o).astype(y_ref.dtype)
        sb_ref[...] = sb_ref[...] * cd_b + _dot_tn((k.astype(F32) * kd_b).astype(BF16), v)
        return carry

    lax.fori_loop(0, N, output_chunk, 0)


def retention_core(proj, lg_f, lg_b, B, T):
    H, dk, dv = RET_HEADS, RET_DK, RET_DV
    C = min(RET_CHUNK, T)
    N = T // C
    v_blk0 = 2 * H * dk // dv
    g_blk0 = v_blk0 + H
    smem = pl.BlockSpec(memory_space=pltpu.SMEM)
    return pl.pallas_call(
        functools.partial(_retention_body, C=C),
        grid=(B, H),
        in_specs=[
            smem, smem,
            pl.BlockSpec((T, dk), lambda b, h: (b, h)),
            pl.BlockSpec((T, dk), lambda b, h: (b, H + h)),
            pl.BlockSpec((T, dv), lambda b, h: (b, v_blk0 + h)),
            pl.BlockSpec((T, dv), lambda b, h: (b, g_blk0 + h)),
        ],
        out_specs=pl.BlockSpec((T, dv), lambda b, h: (b, h)),
        out_shape=jax.ShapeDtypeStruct((B * T, H * dv), BF16),
        scratch_shapes=[pltpu.VMEM((C, C), F32), pltpu.VMEM((N, dk, dv), BF16),
                        pltpu.VMEM((dk, dv), F32), pltpu.VMEM((dk, dv), F32)],
        compiler_params=_cparams("parallel", "parallel"),
        name="retention",
    )(lg_f, lg_b, proj, proj, proj, proj)


def retention_mixer(h2d, gain, w_in, decay_fwd, decay_bwd, w_out, B, T):
    proj = ret_in_proj(h2d, gain, w_in.astype(BF16), T)
    lg_f = jax.nn.log_sigmoid(decay_fwd.astype(F32))
    lg_b = jax.nn.log_sigmoid(decay_bwd.astype(F32))
    y = retention_core(proj, lg_f, lg_b, B, T)
    return matmul_residual(y, w_out.astype(BF16), h2d)


def _na_block_kinds(rows):
    return ((0, 0), (NA_QROWS, 0), (rows - NA_QROWS, rows - NA_KROWS))


def _na_row_pair(rows, r0, ks, a, bb):
    rq, rk = r0 + a, ks + bb
    r_start = min(max(rq - NA_WIN_R // 2, 0), rows - NA_WIN_R)
    return r_start <= rk < r_start + NA_WIN_R, rk - rq + NA_WIN_R - 1


def _na_build_bias(rpb_ref, tile_ref, bias_ref, h, rows):
    W, R, KR = GRID_W, NA_QROWS, NA_KROWS
    n_dr, n_dc = 2 * NA_WIN_R - 1, 2 * NA_WIN_C - 1
    lane = lax.broadcasted_iota(jnp.int32, (W, 2 * W), 1)
    cq = lax.broadcasted_iota(jnp.int32, (W, 2 * W), 0)
    ck = lane & (W - 1)
    hi = lane >= W
    dc = ck - cq + NA_WIN_C - 1
    c_start = jnp.clip(cq - NA_WIN_C // 2, 0, W - NA_WIN_C)
    col_ok = (ck >= c_start) & (ck < c_start + NA_WIN_C)
    base = h * (n_dr * n_dc)
    for t in range(n_dr):
        acc = jnp.zeros((W, 2 * W), F32)
        for j in range(n_dc):
            lo = rpb_ref[base + t * n_dc + j]
            hv = rpb_ref[base + (t + 1) * n_dc + j] if t + 1 < n_dr else 0.0
            acc = jnp.where(dc == j, jnp.where(hi, hv, lo), acc)
        tile_ref[t] = jnp.where(col_ok, acc, MASK_NEG)
    for c, (r0, ks) in enumerate(_na_block_kinds(rows)):
        for a in range(R):
            for pp in range(KR // 2):
                ok0, dr0 = _na_row_pair(rows, r0, ks, a, 2 * pp)
                ok1, dr1 = _na_row_pair(rows, r0, ks, a, 2 * pp + 1)
                if ok0 or ok1:
                    assert 0 <= dr0 < n_dr and (dr1 < n_dr or not ok1)
                    tile = tile_ref[dr0]
                    if not ok0:
                        tile = jnp.where(hi, tile, MASK_NEG)
                    if not ok1:
                        tile = jnp.where(hi, MASK_NEG, tile)
                else:
                    tile = jnp.full((W, 2 * W), MASK_NEG, F32)
                bias_ref[c, a * W:(a + 1) * W, pp * 2 * W:(pp + 1) * 2 * W] = tile


def _na_body(rpb_ref, q_ref, k_ref, v_ref, o_ref, tile_ref, bias_ref, *, rows):
    W, R, KR = GRID_W, NA_QROWS, NA_KROWS
    nb = rows // R
    tq, tk = R * W, KR * W

    @pl.when(pl.program_id(1) == 0)
    def _():
        _na_build_bias(rpb_ref, tile_ref, bias_ref, pl.program_id(0), rows)

    def block(i, carry):
        ks = jnp.clip(R * i - R, 0, rows - KR)
        kind = jnp.where(i == 0, 0, jnp.where(i == nb - 1, 2, 1))
        q0 = pl.multiple_of(i * tq, tq)
        k0 = pl.multiple_of(ks * W, R * W)
        q = q_ref[pl.ds(q0, tq), :]
        k = k_ref[pl.ds(k0, tk), :]
        v = v_ref[pl.ds(k0, tk), :]
        s = _dot_nt(q, k) + bias_ref[kind]
        m = jnp.max(s, axis=-1, keepdims=True)
        p = jnp.exp(s - m)
        p = p / jnp.sum(p, axis=-1, keepdims=True)
        o_ref[pl.ds(q0, tq), :] = _dot(p.astype(BF16), v).astype(o_ref.dtype)
        return carry

    lax.fori_loop(0, nb, block, 0, unroll=2)


def na_core(qkv, rpb, B, T):
    H, dh, W = NA_HEADS, NA_DH, GRID_W
    rows = T // W
    tq, tk = NA_QROWS * W, NA_KROWS * W
    return pl.pallas_call(
        functools.partial(_na_body, rows=rows),
        grid=(H, B),
        in_specs=[
            pl.BlockSpec(memory_space=pltpu.SMEM),
            pl.BlockSpec((T, dh), lambda h, b: (b, h)),
            pl.BlockSpec((T, dh), lambda h, b: (b, H + h)),
            pl.BlockSpec((T, dh), lambda h, b: (b, 2 * H + h)),
        ],
        out_specs=pl.BlockSpec((T, dh), lambda h, b: (b, h)),
        out_shape=jax.ShapeDtypeStruct((B * T, H * dh), BF16),
        scratch_shapes=[pltpu.VMEM((2 * NA_WIN_R - 1, W, 2 * W), F32), pltpu.VMEM((3, tq, tk), F32)],
        compiler_params=_cparams("parallel", "arbitrary"),
        name="neighbourhood_attention",
    )(rpb.astype(F32).reshape(-1), qkv, qkv, qkv)


def na_mixer(h2d, gain, w_qkv, rpb, w_out, B, T):
    H, dh = NA_HEADS, NA_DH
    rows = T // GRID_W
    assert rows % NA_QROWS == 0 and rows >= NA_KROWS and NA_KROWS % 2 == 0
    col_scale = jnp.concatenate([jnp.full((H * dh,), dh ** -0.5, F32), jnp.ones((2 * H * dh,), F32)])
    qkv = norm_matmul(h2d, gain, w_qkv.astype(BF16), col_scale)
    o = na_core(qkv, rpb, B, T)
    return matmul_residual(o, w_out.astype(BF16), h2d)


def _mla_down_body(x_ref, g_ref, wq_ref, wkv_ref, wr_ref, wrs_ref, qn_ref, kvn_ref,
                   cos_ref, sin_ref, cq_ref, ckv_ref, kr_ref):
    xn = _rms(x_ref[...], g_ref[...]).astype(BF16)
    cq_ref[...] = _rms(_dot(xn, wq_ref[...]), qn_ref[...]).astype(cq_ref.dtype)
    ckv_ref[...] = _rms(_dot(xn, wkv_ref[...]), kvn_ref[...]).astype(ckv_ref.dtype)
    kr = _dot(xn, wr_ref[...]) * cos_ref[...] + _dot(xn, wrs_ref[...]) * sin_ref[...]
    kr_ref[...] = kr.astype(kr_ref.dtype)


def _swap_halves(w, group):
    K, N = w.shape
    w = w.reshape(K, N // group, 2, group // 2)
    return w[:, :, ::-1, :].reshape(K, N)


def mla_down(h2d, gain, w_down, q_norm, kv_norm, T, tm=1024):
    M, D = h2d.shape
    tm = _pick(T, tm)
    nT = T // tm
    w = w_down.astype(BF16)
    qr, kvr, rp = MLA_Q_RANK, MLA_KV_RANK, MLA_ROPE
    w_q, w_kv, w_r = w[:, :qr], w[:, qr:qr + kvr], w[:, qr + kvr:]
    pad = jnp.zeros((D, 128 - rp), BF16)
    w_rs = jnp.concatenate([_swap_halves(w_r, rp), pad], axis=1)
    w_r = jnp.concatenate([w_r, pad], axis=1)
    cos, sin = _rope_tables(T, rp)
    zpad = jnp.zeros((T, 128 - rp), F32)
    cos_t = jnp.concatenate([cos, cos, zpad], axis=1)
    sin_t = jnp.concatenate([-sin, sin, zpad], axis=1)
    full = lambda shape: pl.BlockSpec(shape, lambda i: (0, 0))
    return pl.pallas_call(
        _mla_down_body,
        grid=(M // tm,),
        in_specs=[
            pl.BlockSpec((tm, D), lambda i: (i, 0)),
            full((1, D)), full((D, qr)), full((D, kvr)), full((D, 128)), full((D, 128)),
            full((1, qr)), full((1, kvr)),
            pl.BlockSpec((tm, 128), lambda i: (i % nT, 0)),
            pl.BlockSpec((tm, 128), lambda i: (i % nT, 0)),
        ],
        out_specs=[
            pl.BlockSpec((tm, qr), lambda i: (i, 0)),
            pl.BlockSpec((tm, kvr), lambda i: (i, 0)),
            pl.BlockSpec((tm, 128), lambda i: (i, 0)),
        ],
        out_shape=[
            jax.ShapeDtypeStruct((M, qr), BF16),
            jax.ShapeDtypeStruct((M, kvr), BF16),
            jax.ShapeDtypeStruct((M, 128), BF16),
        ],
        compiler_params=_cparams("parallel"),
        name="mla_down",
    )(h2d, gain.reshape(1, D), w_q, w_kv, w_r, w_rs, q_norm.reshape(1, qr), kv_norm.reshape(1, kvr),
      cos_t, sin_t)


def _mla_q_body(c_ref, w_ref, ws_ref, cos_ref, sin_ref, o_ref):
    c = c_ref[...]
    P = MLA_QK_PAD
    scale = (MLA_NOPE + MLA_ROPE) ** -0.5 * LOG2_E
    cos = cos_ref[...]
    sin = sin_ref[...]
    for j in range(o_ref.shape[1] // P):
        sl = slice(j * P, (j + 1) * P)
        x = _dot(c, w_ref[:, sl])
        xs = _dot(c, ws_ref[:, sl])
        o_ref[:, sl] = ((x * cos + xs * sin) * scale).astype(o_ref.dtype)


def mla_q(c_q, w_uq, T, tm=1024, heads_per_step=4):
    M, R = c_q.shape
    H, P, nope, rp = MLA_HEADS, MLA_QK_PAD, MLA_NOPE, MLA_ROPE
    tm = _pick(T, tm)
    nT = T // tm
    w = w_uq.astype(BF16).reshape(R, H, nope + rp)
    zeros = lambda n: jnp.zeros((R, H, n), BF16)
    w_rope = w[:, :, nope:]
    w_main = jnp.concatenate([w[:, :, :nope], w_rope, zeros(P - nope - rp)], axis=2).reshape(R, H * P)
    w_rope_sw = _swap_halves(w_rope.reshape(R, H * rp), rp).reshape(R, H, rp)
    w_swap = jnp.concatenate([zeros(nope), w_rope_sw, zeros(P - nope - rp)], axis=2).reshape(R, H * P)
    cos, sin = _rope_tables(T, rp)
    cos_t = jnp.concatenate([jnp.ones((T, nope), F32), cos, cos, jnp.zeros((T, P - nope - rp), F32)], axis=1)
    sin_t = jnp.concatenate([jnp.zeros((T, nope), F32), -sin, sin, jnp.zeros((T, P - nope - rp), F32)], axis=1)
    tn = heads_per_step * P
    return pl.pallas_call(
        _mla_q_body,
        grid=(M // tm, H * P // tn),
        in_specs=[
            pl.BlockSpec((tm, R), lambda i, j: (i, 0)),
            pl.BlockSpec((R, tn), lambda i, j: (0, j)),
            pl.BlockSpec((R, tn), lambda i, j: (0, j)),
            pl.BlockSpec((tm, P), lambda i, j: (i % nT, 0)),
            pl.BlockSpec((tm, P), lambda i, j: (i % nT, 0)),
        ],
        out_specs=pl.BlockSpec((tm, tn), lambda i, j: (i, j)),
        out_shape=jax.ShapeDtypeStruct((M, H * P), BF16),
        compiler_params=_cparams("parallel", "parallel"),
        name="mla_q",
    )(c_q, w_main, w_swap, cos_t, sin_t)


def _mla_attn_body(q_ref, kn_ref, kr_ref, v_ref, o_ref, kcat_ref, *, tk):
    T = kn_ref.shape[0]

    @pl.when(pl.program_id(2) == 0)
    def _():
        kcat_ref[:, :MLA_NOPE] = kn_ref[...]
        kcat_ref[:, MLA_NOPE:] = kr_ref[...]

    q = q_ref[...]
    tq = q.shape[0]
    nk = T // tk

    def scores(j):
        return _dot_nt(q, kcat_ref[j * tk:(j + 1) * tk, :])

    m = jnp.full((tq, 1), -jnp.inf, F32)
    l = jnp.zeros((tq, 1), F32)
    acc = jnp.zeros((tq, MLA_V), F32)
    s = scores(0)
    for j in range(nk):
        s_next = scores(j + 1) if j + 1 < nk else None
        m_new = jnp.maximum(m, jnp.max(s, axis=-1, keepdims=True))
        alpha = jnp.exp2(m - m_new)
        p = jnp.exp2(s - m_new)
        l = alpha * l + jnp.sum(p, axis=-1, keepdims=True)
        acc = alpha * acc + _dot(p.astype(BF16), v_ref[j * tk:(j + 1) * tk, :])
        m, s = m_new, s_next
    o_ref[...] = (acc / l).astype(o_ref.dtype)


def mla_attention(q, kv, k_rope, B, T):
    H, P = MLA_HEADS, MLA_QK_PAD
    tq, tk = _pick(T, MLA_TQ), _pick(T, MLA_TK)
    nq = T // tq
    return pl.pallas_call(
        functools.partial(_mla_attn_body, tk=tk),
        grid=(B, H, nq),
        in_specs=[
            pl.BlockSpec((tq, P), lambda b, h, i: (b * nq + i, h)),
            pl.BlockSpec((T, MLA_NOPE), lambda b, h, i: (b, h)),
            pl.BlockSpec((T, 128), lambda b, h, i: (b, 0)),
            pl.BlockSpec((T, MLA_V), lambda b, h, i: (b, H + h)),
        ],
        out_specs=pl.BlockSpec((tq, MLA_V), lambda b, h, i: (b * nq + i, h)),
        out_shape=jax.ShapeDtypeStruct((B * T, H * MLA_V), BF16),
        scratch_shapes=[pltpu.VMEM((T, P), BF16)],
        compiler_params=_cparams("parallel", "parallel", "arbitrary"),
        name="mla_attention",
    )(q, kv, k_rope, kv)


def mla_mixer(h2d, gain, w_down, q_norm, kv_norm, w_uq, w_ukv, w_out, B, T):
    H = MLA_HEADS
    c_q, c_kv, k_rope = mla_down(h2d, gain, w_down, q_norm, kv_norm, T)
    q = mla_q(c_q, w_uq, T)
    w = w_ukv.astype(BF16).reshape(MLA_KV_RANK, H, MLA_NOPE + MLA_V)
    w_kv = jnp.concatenate([w[:, :, :MLA_NOPE].reshape(MLA_KV_RANK, H * MLA_NOPE),
                            w[:, :, MLA_NOPE:].reshape(MLA_KV_RANK, H * MLA_V)], axis=1)
    kv = matmul(c_kv, w_kv)
    o = mla_attention(q, kv, k_rope, B, T)
    return matmul_residual(o, w_out.astype(BF16), h2d)


def _conv_gate_body(ug_ref, uv_ref, wg_ref, wv_ref, bg_ref, bv_ref, o_ref, *, rc):
    T = ug_ref.shape[0]
    first = lax.broadcasted_iota(jnp.int32, (rc, 1), 0) == 0
    last = lax.broadcasted_iota(jnp.int32, (rc, 1), 0) == rc - 1

    def conv(u_ref, w_ref, b_ref, r0):
        x = u_ref[r0:r0 + rc, :].astype(F32)
        zero = jnp.zeros((1, x.shape[1]), F32)
        before = u_ref[r0 - 16:r0, :].astype(F32)[15:16, :] if r0 > 0 else zero
        after = u_ref[r0 + rc:r0 + rc + 16, :].astype(F32)[0:1, :] if r0 + rc < T else zero
        prev = jnp.where(first, before, pltpu.roll(x, 1, axis=0))
        nxt = jnp.where(last, after, pltpu.roll(x, rc - 1, axis=0))
        return prev * w_ref[0:1, :] + x * w_ref[1:2, :] + nxt * w_ref[2:3, :] + b_ref[...]

    for r0 in range(0, T, rc):
        g = conv(ug_ref, wg_ref, bg_ref, r0)
        val = conv(uv_ref, wv_ref, bv_ref, r0)
        o_ref[r0:r0 + rc, :] = (g * jax.nn.sigmoid(g) * val).astype(o_ref.dtype)


def conv_gate(u, conv_w, conv_b, B, T, tc=256, rc=512):
    F = u.shape[1] // 2
    tc = _pick(F, tc)
    rc = _pick(T, rc)
    nj = F // tc
    b2 = conv_b.reshape(1, 2 * F).astype(F32)
    w = conv_w.astype(F32)
    return pl.pallas_call(
        functools.partial(_conv_gate_body, rc=rc),
        grid=(B, nj),
        in_specs=[
            pl.BlockSpec((T, tc), lambda b, j: (b, j)),
            pl.BlockSpec((T, tc), lambda b, j: (b, nj + j)),
            pl.BlockSpec((CONV_W, tc), lambda b, j: (0, j)),
            pl.BlockSpec((CONV_W, tc), lambda b, j: (0, nj + j)),
            pl.BlockSpec((1, tc), lambda b, j: (0, j)),
            pl.BlockSpec((1, tc), lambda b, j: (0, nj + j)),
        ],
        out_specs=pl.BlockSpec((T, tc), lambda b, j: (b, j)),
        out_shape=jax.ShapeDtypeStruct((B * T, F), BF16),
        compiler_params=_cparams("parallel", "parallel"),
        name="conv_gate",
    )(u, u, w, w, b2, b2)


def conv_ffn(h2d, gain, w_up, conv_w, conv_b, w_down, B, T, final_gain=None):
    u = norm_matmul(h2d, gain, w_up.astype(BF16))
    a = conv_gate(u, conv_w, conv_b, B, T)
    return matmul_residual(a, w_down.astype(BF16), h2d, final_gain=final_gain)


def kernel(x, l0_attn_norm, l0_ret_w_in, l0_ret_decay_fwd, l0_ret_decay_bwd, l0_ret_w_out, l0_ffn_norm, l0_ffn_w_up, l0_ffn_conv_w, l0_ffn_conv_b, l0_ffn_w_down, l1_attn_norm, l1_na_w_qkv, l1_na_rpb, l1_na_w_out, l1_ffn_norm, l1_ffn_w_up, l1_ffn_conv_w, l1_ffn_conv_b, l1_ffn_w_down, l2_attn_norm, l2_mla_w_down, l2_mla_q_norm, l2_mla_kv_norm, l2_mla_w_uq, l2_mla_w_ukv, l2_mla_w_out, l2_ffn_norm, l2_ffn_w_up, l2_ffn_conv_w, l2_ffn_conv_b, l2_ffn_w_down, l3_attn_norm, l3_ret_w_in, l3_ret_decay_fwd, l3_ret_decay_bwd, l3_ret_w_out, l3_ffn_norm, l3_ffn_w_up, l3_ffn_conv_w, l3_ffn_conv_b, l3_ffn_w_down, final_norm):
    B, T, D = x.shape
    h = x.reshape(B * T, D)
    h = retention_mixer(h, l0_attn_norm, l0_ret_w_in, l0_ret_decay_fwd, l0_ret_decay_bwd, l0_ret_w_out, B, T)
    h = conv_ffn(h, l0_ffn_norm, l0_ffn_w_up, l0_ffn_conv_w, l0_ffn_conv_b, l0_ffn_w_down, B, T)
    h = na_mixer(h, l1_attn_norm, l1_na_w_qkv, l1_na_rpb, l1_na_w_out, B, T)
    h = conv_ffn(h, l1_ffn_norm, l1_ffn_w_up, l1_ffn_conv_w, l1_ffn_conv_b, l1_ffn_w_down, B, T)
    h = mla_mixer(h, l2_attn_norm, l2_mla_w_down, l2_mla_q_norm, l2_mla_kv_norm, l2_mla_w_uq, l2_mla_w_ukv, l2_mla_w_out, B, T)
    h = conv_ffn(h, l2_ffn_norm, l2_ffn_w_up, l2_ffn_conv_w, l2_ffn_conv_b, l2_ffn_w_down, B, T)
    h = retention_mixer(h, l3_attn_norm, l3_ret_w_in, l3_ret_decay_fwd, l3_ret_decay_bwd, l3_ret_w_out, B, T)
    h = conv_ffn(h, l3_ffn_norm, l3_ffn_w_up, l3_ffn_conv_w, l3_ffn_conv_b, l3_ffn_w_down, B, T,
                 final_gain=final_norm)
    return h.reshape(B, T, D)
```

```python
import functools

import jax
import jax.numpy as jnp
from jax import lax
from jax.experimental import pallas as pl
from jax.experimental.pallas import tpu as pltpu

F32 = jnp.float32
BF16 = jnp.bfloat16

GRID_W = 64
ROPE_BASE = 10000.0
NORM_EPS = 1e-6
GN_EPS = 1e-5

RET_HEADS = 8
RET_DK = 256
RET_DV = 512
RET_CHUNK = 256

NA_HEADS = 16
NA_DH = 128
NA_WIN_R = 8
NA_WIN_C = 16
NA_QROWS = 4
NA_KROWS = NA_QROWS + NA_WIN_R

MLA_HEADS = 16
MLA_Q_RANK = 512
MLA_KV_RANK = 512
MLA_NOPE = 128
MLA_ROPE = 64
MLA_V = 128
MLA_QK_PAD = 256
MLA_TQ = 512
MLA_TK = 512
LOG2_E = 1.4426950408889634

CONV_W = 3
MASK_NEG = -1e30

VMEM_LIMIT = 56 * 1024 * 1024


def _cparams(*sem):
    return pltpu.CompilerParams(dimension_semantics=sem, vmem_limit_bytes=VMEM_LIMIT)


def _dot(a, b):
    return jnp.dot(a, b, preferred_element_type=F32)


def _dot_nt(a, b):
    return lax.dot_general(a, b, (((1,), (1,)), ((), ())), preferred_element_type=F32)


def _dot_tn(a, b):
    return lax.dot_general(a, b, (((0,), (0,)), ((), ())), preferred_element_type=F32)


def _rms(x, g):
    return x * lax.rsqrt(jnp.mean(x * x, axis=-1, keepdims=True) + NORM_EPS) * g


def _pick(n, pref):
    if n <= pref:
        return n
    t = pref
    while n % t:
        t //= 2
    return t


def _rmsnorm_cast_body(x_ref, g_ref, o_ref):
    o_ref[...] = _rms(x_ref[...], g_ref[...]).astype(o_ref.dtype)


def rmsnorm_cast(x, g, tm=512):
    M, K = x.shape
    tm = _pick(M, tm)
    return pl.pallas_call(
        _rmsnorm_cast_body,
        grid=(M // tm,),
        in_specs=[pl.BlockSpec((tm, K), lambda i: (i, 0)), pl.BlockSpec((1, K), lambda i: (0, 0))],
        out_specs=pl.BlockSpec((tm, K), lambda i: (i, 0)),
        out_shape=jax.ShapeDtypeStruct((M, K), BF16),
        compiler_params=_cparams("parallel"),
        name="rmsnorm_cast",
    )(x, g.reshape(1, K))


def _matmul_res_body(a_ref, w_ref, r_ref, g_ref, o_ref, *xn_ref, final):
    k = pl.program_id(1)

    @pl.when(k == 0)
    def _():
        o_ref[...] = r_ref[...]

    o_ref[...] += _dot(a_ref[...], w_ref[...])

    @pl.when(k == pl.num_programs(1) - 1)
    def _():
        y = _rms(o_ref[...], g_ref[...])
        if final:
            o_ref[...] = y
        else:
            xn_ref[0][...] = y.astype(BF16)


def matmul_residual(a, w, r, gain, final=False, tm=1024, tk=512):
    M, K = a.shape
    N = w.shape[1]
    tm, tk = _pick(M, tm), _pick(K, tk)
    row_block = pl.BlockSpec((tm, N), lambda i, k: (i, 0))
    out_specs, out_shape = row_block, jax.ShapeDtypeStruct((M, N), F32)
    if not final:
        out_specs, out_shape = [row_block, row_block], [out_shape, jax.ShapeDtypeStruct((M, N), BF16)]
    return pl.pallas_call(
        functools.partial(_matmul_res_body, final=final),
        grid=(M // tm, K // tk),
        in_specs=[
            pl.BlockSpec((tm, tk), lambda i, k: (i, k)),
            pl.BlockSpec((tk, N), lambda i, k: (k, 0)),
            row_block,
            pl.BlockSpec((1, N), lambda i, k: (0, 0)),
        ],
        out_specs=out_specs,
        out_shape=out_shape,
        compiler_params=_cparams("parallel", "arbitrary"),
        name="matmul_residual",
    )(a, w, r, gain.reshape(1, N))


def _matmul_body(a_ref, w_ref, cs_ref, o_ref):
    o_ref[...] = (_dot(a_ref[...], w_ref[...]) * cs_ref[...]).astype(o_ref.dtype)


def matmul(a, w, col_scale=None, tm=1024, tn=1024):
    M, K = a.shape
    N = w.shape[1]
    tm, tn = _pick(M, tm), _pick(N, tn)
    if col_scale is None:
        col_scale = jnp.ones((N,), F32)
    return pl.pallas_call(
        _matmul_body,
        grid=(M // tm, N // tn),
        in_specs=[
            pl.BlockSpec((tm, K), lambda i, j: (i, 0)),
            pl.BlockSpec((K, tn), lambda i, j: (0, j)),
            pl.BlockSpec((1, tn), lambda i, j: (0, j)),
        ],
        out_specs=pl.BlockSpec((tm, tn), lambda i, j: (i, j)),
        out_shape=jax.ShapeDtypeStruct((M, N), BF16),
        compiler_params=_cparams("parallel", "parallel"),
        name="matmul",
    )(a, w, col_scale.reshape(1, N))


def _rope_tables(T, d):
    half = d // 2
    inv = ROPE_BASE ** (-jnp.arange(half, dtype=F32) * 2.0 / d)
    ang = jnp.arange(T, dtype=F32)[:, None] * inv[None, :]
    return jnp.cos(ang), jnp.sin(ang)


def _ret_in_body(x_ref, w_ref, cos_ref, sin_ref, o_ref, *, n_q_tiles):
    j = pl.program_id(1)
    acc = _dot(x_ref[...], w_ref[...])

    @pl.when(j < 2 * n_q_tiles)
    def _():
        cos = cos_ref[...]
        sin = sin_ref[...]
        scale = jnp.where(j >= n_q_tiles, RET_DK ** -0.5, 1.0).astype(F32)
        half = RET_DK // 2
        for c0 in range(0, acc.shape[1], RET_DK):
            x1 = acc[:, c0:c0 + half]
            x2 = acc[:, c0 + half:c0 + RET_DK]
            o_ref[:, c0:c0 + half] = ((x1 * cos - x2 * sin) * scale).astype(o_ref.dtype)
            o_ref[:, c0 + half:c0 + RET_DK] = ((x1 * sin + x2 * cos) * scale).astype(o_ref.dtype)

    @pl.when(j >= 2 * n_q_tiles)
    def _():
        o_ref[...] = acc.astype(o_ref.dtype)


def ret_in_proj(x, w, T, tm=1024, tn=1024):
    M, K = x.shape
    N = w.shape[1]
    tm, tn = _pick(T, tm), _pick(N, tn)
    nT = T // tm
    qk_w = RET_HEADS * RET_DK
    assert tn % RET_DK == 0 and qk_w % tn == 0
    cos, sin = _rope_tables(T, RET_DK)
    return pl.pallas_call(
        functools.partial(_ret_in_body, n_q_tiles=qk_w // tn),
        grid=(M // tm, N // tn),
        in_specs=[
            pl.BlockSpec((tm, K), lambda i, j: (i, 0)),
            pl.BlockSpec((K, tn), lambda i, j: (0, j)),
            pl.BlockSpec((tm, RET_DK // 2), lambda i, j: (i % nT, 0)),
            pl.BlockSpec((tm, RET_DK // 2), lambda i, j: (i % nT, 0)),
        ],
        out_specs=pl.BlockSpec((tm, tn), lambda i, j: (i, j)),
        out_shape=jax.ShapeDtypeStruct((M, N), BF16),
        compiler_params=_cparams("parallel", "parallel"),
        name="ret_in_proj",
    )(x, w, cos, sin)


def _retention_body(lgf_ref, lgb_ref, q_ref, k_ref, v_ref, g_ref, y_ref,
                    decay_ref, sf_all_ref, sf_ref, sb_ref, *, C):
    h = pl.program_id(1)
    T = q_ref.shape[0]
    N = T // C
    lg_f = lgf_ref[h]
    lg_b = lgb_ref[h]

    idx = lax.broadcasted_iota(jnp.int32, (C, 1), 0).astype(F32)
    kd_f = jnp.exp((C - 1.0 - idx) * lg_f)
    kd_b = jnp.exp(idx * lg_b)
    qd_f = jnp.exp((idx + 1.0) * lg_f)
    qd_b = jnp.exp((C - idx) * lg_b)
    chunk_len = jnp.full((1, 1), C, F32)
    cd_f = jnp.exp(chunk_len * lg_f)
    cd_b = jnp.exp(chunk_len * lg_b)

    row = lax.broadcasted_iota(jnp.int32, (C, C), 0)
    col = lax.broadcasted_iota(jnp.int32, (C, C), 1)
    decay_ref[...] = jnp.exp(jnp.where(row >= col, (row - col).astype(F32) * lg_f,
                                       (col - row).astype(F32) * lg_b))

    def rows(n):
        return pl.ds(pl.multiple_of(n * C, C), C)

    sf_ref[...] = jnp.zeros_like(sf_ref)
    sb_ref[...] = jnp.zeros_like(sb_ref)

    def forward_state(n, carry):
        k = k_ref[rows(n), :].astype(F32)
        sf_all_ref[n] = sf_ref[...].astype(BF16)
        sf_ref[...] = sf_ref[...] * cd_f + _dot_tn((k * kd_f).astype(BF16), v_ref[rows(n), :])
        return carry

    lax.fori_loop(0, N, forward_state, 0)

    def output_chunk(i, carry):
        n = N - 1 - i
        q = q_ref[rows(n), :]
        k = k_ref[rows(n), :]
        v = v_ref[rows(n), :]
        qf = q.astype(F32)
        scores = (_dot_nt(q, k) * decay_ref[...]).astype(BF16)
        o = (_dot(scores, v)
             + _dot((qf * qd_f).astype(BF16), sf_all_ref[n])
             + _dot((qf * qd_b).astype(BF16), sb_ref[...].astype(BF16)))
        mu = jnp.mean(o, axis=-1, keepdims=True)
        d = o - mu
        var = jnp.mean(d * d, axis=-1, keepdims=True)
        o = d * lax.rsqrt(var + GN_EPS)
        g = g_ref[rows(n), :].astype(F32)
        y_ref[rows(n), :] = (g * jax.nn.sigmoid(g) * o).astype(y_ref.dtype)
        sb_ref[...] = sb_ref[...] * cd_b + _dot_tn((k.astype(F32) * kd_b).astype(BF16), v)
        return carry

    lax.fori_loop(0, N, output_chunk, 0, unroll=2)


def retention_core(proj, lg_f, lg_b, B, T):
    H, dk, dv = RET_HEADS, RET_DK, RET_DV
    C = min(RET_CHUNK, T)
    N = T // C
    v_blk0 = 2 * H * dk // dv
    g_blk0 = v_blk0 + H
    smem = pl.BlockSpec(memory_space=pltpu.SMEM)
    return pl.pallas_call(
        functools.partial(_retention_body, C=C),
        grid=(B, H),
        in_specs=[
            smem, smem,
            pl.BlockSpec((T, dk), lambda b, h: (b, h)),
            pl.BlockSpec((T, dk), lambda b, h: (b, H + h)),
            pl.BlockSpec((T, dv), lambda b, h: (b, v_blk0 + h)),
            pl.BlockSpec((T, dv), lambda b, h: (b, g_blk0 + h)),
        ],
        out_specs=pl.BlockSpec((T, dv), lambda b, h: (b, h)),
        out_shape=jax.ShapeDtypeStruct((B * T, H * dv), BF16),
        scratch_shapes=[pltpu.VMEM((C, C), F32), pltpu.VMEM((N, dk, dv), BF16),
                        pltpu.VMEM((dk, dv), F32), pltpu.VMEM((dk, dv), F32)],
        compiler_params=_cparams("parallel", "parallel"),
        name="retention",
    )(lg_f, lg_b, proj, proj, proj, proj)


def retention_mixer(h, xn, w_in, decay_fwd, decay_bwd, w_out, next_gain, B, T):
    proj = ret_in_proj(xn, w_in.astype(BF16), T)
    lg_f = jax.nn.log_sigmoid(decay_fwd.astype(F32))
    lg_b = jax.nn.log_sigmoid(decay_bwd.astype(F32))
    y = retention_core(proj, lg_f, lg_b, B, T)
    return matmul_residual(y, w_out.astype(BF16), h, next_gain)


def _na_block_kinds(rows):
    return ((0, 0), (NA_QROWS, 0), (rows - NA_QROWS, rows - NA_KROWS))


def _na_row_pair(rows, r0, ks, a, bb):
    rq, rk = r0 + a, ks + bb
    r_start = min(max(rq - NA_WIN_R // 2, 0), rows - NA_WIN_R)
    return r_start <= rk < r_start + NA_WIN_R, rk - rq + NA_WIN_R - 1


def _na_build_bias(rpb_ref, tile_ref, bias_ref, h, rows):
    W, R, KR = GRID_W, NA_QROWS, NA_KROWS
    n_dr, n_dc = 2 * NA_WIN_R - 1, 2 * NA_WIN_C - 1
    lane = lax.broadcasted_iota(jnp.int32, (W, 2 * W), 1)
    cq = lax.broadcasted_iota(jnp.int32, (W, 2 * W), 0)
    ck = lane & (W - 1)
    hi = lane >= W
    dc = ck - cq + NA_WIN_C - 1
    c_start = jnp.clip(cq - NA_WIN_C // 2, 0, W - NA_WIN_C)
    col_ok = (ck >= c_start) & (ck < c_start + NA_WIN_C)
    base = h * (n_dr * n_dc)
    for t in range(n_dr):
        acc = jnp.zeros((W, 2 * W), F32)
        for j in range(n_dc):
            lo = rpb_ref[base + t * n_dc + j]
            hv = rpb_ref[base + (t + 1) * n_dc + j] if t + 1 < n_dr else 0.0
            acc = jnp.where(dc == j, jnp.where(hi, hv, lo), acc)
        tile_ref[t] = jnp.where(col_ok, acc, MASK_NEG)
    for c, (r0, ks) in enumerate(_na_block_kinds(rows)):
        for a in range(R):
            for pp in range(KR // 2):
                ok0, dr0 = _na_row_pair(rows, r0, ks, a, 2 * pp)
                ok1, dr1 = _na_row_pair(rows, r0, ks, a, 2 * pp + 1)
                if ok0 or ok1:
                    assert 0 <= dr0 < n_dr and (dr1 < n_dr or not ok1)
                    tile = tile_ref[dr0]
                    if not ok0:
                        tile = jnp.where(hi, tile, MASK_NEG)
                    if not ok1:
                        tile = jnp.where(hi, MASK_NEG, tile)
                else:
                    tile = jnp.full((W, 2 * W), MASK_NEG, F32)
                bias_ref[c, a * W:(a + 1) * W, pp * 2 * W:(pp + 1) * 2 * W] = tile


def _na_body(rpb_ref, q_ref, k_ref, v_ref, o_ref, tile_ref, bias_ref, *, rows):
    W, R, KR = GRID_W, NA_QROWS, NA_KROWS
    nb = rows // R
    tq, tk = R * W, KR * W

    @pl.when(pl.program_id(1) == 0)
    def _():
        _na_build_bias(rpb_ref, tile_ref, bias_ref, pl.program_id(0), rows)

    def key_rows(i):
        k0 = min(max(R * i - R, 0), rows - KR) * W
        return slice(k0, k0 + tk)

    def scores(i):
        kind = 0 if i == 0 else (2 if i == nb - 1 else 1)
        return _dot_nt(q_ref[i * tq:(i + 1) * tq, :], k_ref[key_rows(i), :]) + bias_ref[kind]

    s = scores(0)
    for i in range(nb):
        s_next = scores(i + 1) if i + 1 < nb else None
        m = jnp.max(s, axis=-1, keepdims=True)
        p = jnp.exp(s - m)
        l = jnp.sum(p, axis=-1, keepdims=True)
        o = _dot(p.astype(BF16), v_ref[key_rows(i), :]) / l
        o_ref[i * tq:(i + 1) * tq, :] = o.astype(o_ref.dtype)
        s = s_next


def na_core(qkv, rpb, B, T):
    H, dh, W = NA_HEADS, NA_DH, GRID_W
    rows = T // W
    tq, tk = NA_QROWS * W, NA_KROWS * W
    return pl.pallas_call(
        functools.partial(_na_body, rows=rows),
        grid=(H, B),
        in_specs=[
            pl.BlockSpec(memory_space=pltpu.SMEM),
            pl.BlockSpec((T, dh), lambda h, b: (b, h)),
            pl.BlockSpec((T, dh), lambda h, b: (b, H + h)),
            pl.BlockSpec((T, dh), lambda h, b: (b, 2 * H + h)),
        ],
        out_specs=pl.BlockSpec((T, dh), lambda h, b: (b, h)),
        out_shape=jax.ShapeDtypeStruct((B * T, H * dh), BF16),
        scratch_shapes=[pltpu.VMEM((2 * NA_WIN_R - 1, W, 2 * W), F32), pltpu.VMEM((3, tq, tk), F32)],
        compiler_params=_cparams("parallel", "arbitrary"),
        name="neighbourhood_attention",
    )(rpb.astype(F32).reshape(-1), qkv, qkv, qkv)


def na_mixer(h, xn, w_qkv, rpb, w_out, next_gain, B, T):
    H, dh = NA_HEADS, NA_DH
    rows = T // GRID_W
    assert rows % NA_QROWS == 0 and rows >= NA_KROWS and NA_KROWS % 2 == 0
    col_scale = jnp.concatenate([jnp.full((H * dh,), dh ** -0.5, F32), jnp.ones((2 * H * dh,), F32)])
    qkv = matmul(xn, w_qkv.astype(BF16), col_scale)
    o = na_core(qkv, rpb, B, T)
    return matmul_residual(o, w_out.astype(BF16), h, next_gain)


def _mla_down_body(x_ref, wq_ref, wkv_ref, wr_ref, wrs_ref, qn_ref, kvn_ref,
                   cos_ref, sin_ref, cq_ref, ckv_ref, kr_ref):
    xn = x_ref[...]
    cq_ref[...] = _rms(_dot(xn, wq_ref[...]), qn_ref[...]).astype(cq_ref.dtype)
    ckv_ref[...] = _rms(_dot(xn, wkv_ref[...]), kvn_ref[...]).astype(ckv_ref.dtype)
    kr = _dot(xn, wr_ref[...]) * cos_ref[...] + _dot(xn, wrs_ref[...]) * sin_ref[...]
    kr_ref[...] = kr.astype(kr_ref.dtype)


def _swap_halves(w, group):
    K, N = w.shape
    w = w.reshape(K, N // group, 2, group // 2)
    return w[:, :, ::-1, :].reshape(K, N)


def mla_down(xn, w_down, q_norm, kv_norm, T, tm=1024):
    M, D = xn.shape
    tm = _pick(T, tm)
    nT = T // tm
    w = w_down.astype(BF16)
    qr, kvr, rp = MLA_Q_RANK, MLA_KV_RANK, MLA_ROPE
    w_q, w_kv, w_r = w[:, :qr], w[:, qr:qr + kvr], w[:, qr + kvr:]
    pad = jnp.zeros((D, 128 - rp), BF16)
    w_rs = jnp.concatenate([_swap_halves(w_r, rp), pad], axis=1)
    w_r = jnp.concatenate([w_r, pad], axis=1)
    cos, sin = _rope_tables(T, rp)
    zpad = jnp.zeros((T, 128 - rp), F32)
    cos_t = jnp.concatenate([cos, cos, zpad], axis=1)
    sin_t = jnp.concatenate([-sin, sin, zpad], axis=1)
    full = lambda shape: pl.BlockSpec(shape, lambda i: (0, 0))
    return pl.pallas_call(
        _mla_down_body,
        grid=(M // tm,),
        in_specs=[
            pl.BlockSpec((tm, D), lambda i: (i, 0)),
            full((D, qr)), full((D, kvr)), full((D, 128)), full((D, 128)),
            full((1, qr)), full((1, kvr)),
            pl.BlockSpec((tm, 128), lambda i: (i % nT, 0)),
            pl.BlockSpec((tm, 128), lambda i: (i % nT, 0)),
        ],
        out_specs=[
            pl.BlockSpec((tm, qr), lambda i: (i, 0)),
            pl.BlockSpec((tm, kvr), lambda i: (i, 0)),
            pl.BlockSpec((tm, 128), lambda i: (i, 0)),
        ],
        out_shape=[
            jax.ShapeDtypeStruct((M, qr), BF16),
            jax.ShapeDtypeStruct((M, kvr), BF16),
            jax.ShapeDtypeStruct((M, 128), BF16),
        ],
        compiler_params=_cparams("parallel"),
        name="mla_down",
    )(xn, w_q, w_kv, w_r, w_rs, q_norm.reshape(1, qr), kv_norm.reshape(1, kvr), cos_t, sin_t)


def _mla_q_body(c_ref, w_ref, ws_ref, cos_ref, sin_ref, o_ref):
    c = c_ref[...]
    P = MLA_QK_PAD
    scale = (MLA_NOPE + MLA_ROPE) ** -0.5 * LOG2_E
    cos = cos_ref[...]
    sin = sin_ref[...]
    for j in range(o_ref.shape[1] // P):
        sl = slice(j * P, (j + 1) * P)
        x = _dot(c, w_ref[:, sl])
        xs = _dot(c, ws_ref[:, sl])
        o_ref[:, sl] = ((x * cos + xs * sin) * scale).astype(o_ref.dtype)


def mla_q(c_q, w_uq, T, tm=1024, heads_per_step=4):
    M, R = c_q.shape
    H, P, nope, rp = MLA_HEADS, MLA_QK_PAD, MLA_NOPE, MLA_ROPE
    tm = _pick(T, tm)
    nT = T // tm
    w = w_uq.astype(BF16).reshape(R, H, nope + rp)
    zeros = lambda n: jnp.zeros((R, H, n), BF16)
    w_rope = w[:, :, nope:]
    w_main = jnp.concatenate([w[:, :, :nope], w_rope, zeros(P - nope - rp)], axis=2).reshape(R, H * P)
    w_rope_sw = _swap_halves(w_rope.reshape(R, H * rp), rp).reshape(R, H, rp)
    w_swap = jnp.concatenate([zeros(nope), w_rope_sw, zeros(P - nope - rp)], axis=2).reshape(R, H * P)
    cos, sin = _rope_tables(T, rp)
    cos_t = jnp.concatenate([jnp.ones((T, nope), F32), cos, cos, jnp.zeros((T, P - nope - rp), F32)], axis=1)
    sin_t = jnp.concatenate([jnp.zeros((T, nope), F32), -sin, sin, jnp.zeros((T, P - nope - rp), F32)], axis=1)
    tn = heads_per_step * P
    return pl.pallas_call(
        _mla_q_body,
        grid=(M // tm, H * P // tn),
        in_specs=[
            pl.BlockSpec((tm, R), lambda i, j: (i, 0)),
            pl.BlockSpec((R, tn), lambda i, j: (0, j)),
            pl.BlockSpec((R, tn), lambda i, j: (0, j)),
            pl.BlockSpec((tm, P), lambda i, j: (i % nT, 0)),
            pl.BlockSpec((tm, P), lambda i, j: (i % nT, 0)),
        ],
        out_specs=pl.BlockSpec((tm, tn), lambda i, j: (i, j)),
        out_shape=jax.ShapeDtypeStruct((M, H * P), BF16),
        compiler_params=_cparams("parallel", "parallel"),
        name="mla_q",
    )(c_q, w_main, w_swap, cos_t, sin_t)


def _mla_attn_body(q_ref, kn_ref, kr_ref, v_ref, o_ref, kcat_ref, *, tk):
    T = kn_ref.shape[0]

    @pl.when(pl.program_id(2) == 0)
    def _():
        kcat_ref[:, :MLA_NOPE] = kn_ref[...]
        kcat_ref[:, MLA_NOPE:] = kr_ref[...]

    q = q_ref[...]
    tq = q.shape[0]
    nk = T // tk

    def scores(j):
        return _dot_nt(q, kcat_ref[j * tk:(j + 1) * tk, :])

    m = jnp.full((tq, 1), -jnp.inf, F32)
    l = jnp.zeros((tq, 1), F32)
    acc = jnp.zeros((tq, MLA_V), F32)
    s = scores(0)
    for j in range(nk):
        s_next = scores(j + 1) if j + 1 < nk else None
        m_new = jnp.maximum(m, jnp.max(s, axis=-1, keepdims=True))
        alpha = jnp.exp2(m - m_new)
        p = jnp.exp2(s - m_new)
        l = alpha * l + jnp.sum(p, axis=-1, keepdims=True)
        acc = alpha * acc + _dot(p.astype(BF16), v_ref[j * tk:(j + 1) * tk, :])
        m, s = m_new, s_next
    o_ref[...] = (acc / l).astype(o_ref.dtype)


def mla_attention(q, kv, k_rope, B, T):
    H, P = MLA_HEADS, MLA_QK_PAD
    tq, tk = _pick(T, MLA_TQ), _pick(T, MLA_TK)
    nq = T // tq
    return pl.pallas_call(
        functools.partial(_mla_attn_body, tk=tk),
        grid=(B, H, nq),
        in_specs=[
            pl.BlockSpec((tq, P), lambda b, h, i: (b * nq + i, h)),
            pl.BlockSpec((T, MLA_NOPE), lambda b, h, i: (b, h)),
            pl.BlockSpec((T, 128), lambda b, h, i: (b, 0)),
            pl.BlockSpec((T, MLA_V), lambda b, h, i: (b, H + h)),
        ],
        out_specs=pl.BlockSpec((tq, MLA_V), lambda b, h, i: (b * nq + i, h)),
        out_shape=jax.ShapeDtypeStruct((B * T, H * MLA_V), BF16),
        scratch_shapes=[pltpu.VMEM((T, P), BF16)],
        compiler_params=_cparams("parallel", "parallel", "arbitrary"),
        name="mla_attention",
    )(q, kv, k_rope, kv)


def mla_mixer(h, xn, w_down, q_norm, kv_norm, w_uq, w_ukv, w_out, next_gain, B, T):
    H = MLA_HEADS
    c_q, c_kv, k_rope = mla_down(xn, w_down, q_norm, kv_norm, T)
    q = mla_q(c_q, w_uq, T)
    w = w_ukv.astype(BF16).reshape(MLA_KV_RANK, H, MLA_NOPE + MLA_V)
    w_kv = jnp.concatenate([w[:, :, :MLA_NOPE].reshape(MLA_KV_RANK, H * MLA_NOPE),
                            w[:, :, MLA_NOPE:].reshape(MLA_KV_RANK, H * MLA_V)], axis=1)
    kv = matmul(c_kv, w_kv)
    o = mla_attention(q, kv, k_rope, B, T)
    return matmul_residual(o, w_out.astype(BF16), h, next_gain)


def _ffn_up_body(x_ref, wg_ref, wv_ref, cwg_ref, cwv_ref, bg_ref, bv_ref, o_ref, *, rc):
    T, tn = o_ref.shape
    n = T // rc
    first = lax.broadcasted_iota(jnp.int32, (rc, 1), 0) == 0
    last = lax.broadcasted_iota(jnp.int32, (rc, 1), 0) == rc - 1
    zero = jnp.zeros((1, tn), F32)

    def project(c):
        x = x_ref[c * rc:(c + 1) * rc, :]
        return _dot(x, wg_ref[...]), _dot(x, wv_ref[...])

    def conv(u, before, after, w_ref, b_ref):
        prev = jnp.where(first, before, pltpu.roll(u, 1, axis=0))
        nxt = jnp.where(last, after, pltpu.roll(u, rc - 1, axis=0))
        return prev * w_ref[0:1, :] + u * w_ref[1:2, :] + nxt * w_ref[2:3, :] + b_ref[...]

    cur = project(0)
    before = (zero, zero)
    for c in range(n):
        nxt = project(c + 1) if c + 1 < n else None
        after = (nxt[0][0:1, :], nxt[1][0:1, :]) if nxt is not None else (zero, zero)
        g = conv(cur[0], before[0], after[0], cwg_ref, bg_ref)
        val = conv(cur[1], before[1], after[1], cwv_ref, bv_ref)
        o_ref[c * rc:(c + 1) * rc, :] = (g * jax.nn.sigmoid(g) * val).astype(o_ref.dtype)
        before = (cur[0][rc - 1:rc, :], cur[1][rc - 1:rc, :])
        cur = nxt


def ffn_up_conv_gate(xn, w_up, conv_w, conv_b, B, T, tn=512, rc=512):
    D = xn.shape[1]
    F = w_up.shape[1] // 2
    tn, rc = _pick(F, tn), _pick(T, rc)
    nj = F // tn
    b2 = conv_b.reshape(1, 2 * F).astype(F32)
    cw = conv_w.astype(F32)
    return pl.pallas_call(
        functools.partial(_ffn_up_body, rc=rc),
        grid=(B, nj),
        in_specs=[
            pl.BlockSpec((T, D), lambda b, j: (b, 0), pipeline_mode=pl.Buffered(1)),
            pl.BlockSpec((D, tn), lambda b, j: (0, j)),
            pl.BlockSpec((D, tn), lambda b, j: (0, nj + j)),
            pl.BlockSpec((CONV_W, tn), lambda b, j: (0, j)),
            pl.BlockSpec((CONV_W, tn), lambda b, j: (0, nj + j)),
            pl.BlockSpec((1, tn), lambda b, j: (0, j)),
            pl.BlockSpec((1, tn), lambda b, j: (0, nj + j)),
        ],
        out_specs=pl.BlockSpec((T, tn), lambda b, j: (b, j)),
        out_shape=jax.ShapeDtypeStruct((B * T, F), BF16),
        compiler_params=_cparams("parallel", "arbitrary"),
        name="ffn_up_conv_gate",
    )(xn, w_up, w_up, cw, cw, b2, b2)


def conv_ffn(h, xn, w_up, conv_w, conv_b, w_down, next_gain, B, T, final=False):
    a = ffn_up_conv_gate(xn, w_up.astype(BF16), conv_w, conv_b, B, T)
    return matmul_residual(a, w_down.astype(BF16), h, next_gain, final=final)


def kernel(x, l0_attn_norm, l0_ret_w_in, l0_ret_decay_fwd, l0_ret_decay_bwd, l0_ret_w_out, l0_ffn_norm, l0_ffn_w_up, l0_ffn_conv_w, l0_ffn_conv_b, l0_ffn_w_down, l1_attn_norm, l1_na_w_qkv, l1_na_rpb, l1_na_w_out, l1_ffn_norm, l1_ffn_w_up, l1_ffn_conv_w, l1_ffn_conv_b, l1_ffn_w_down, l2_attn_norm, l2_mla_w_down, l2_mla_q_norm, l2_mla_kv_norm, l2_mla_w_uq, l2_mla_w_ukv, l2_mla_w_out, l2_ffn_norm, l2_ffn_w_up, l2_ffn_conv_w, l2_ffn_conv_b, l2_ffn_w_down, l3_attn_norm, l3_ret_w_in, l3_ret_decay_fwd, l3_ret_decay_bwd, l3_ret_w_out, l3_ffn_norm, l3_ffn_w_up, l3_ffn_conv_w, l3_ffn_conv_b, l3_ffn_w_down, final_norm):
    B, T, D = x.shape
    h = x.reshape(B * T, D)
    xn = rmsnorm_cast(h, l0_attn_norm)
    h, xn = retention_mixer(h, xn, l0_ret_w_in, l0_ret_decay_fwd, l0_ret_decay_bwd, l0_ret_w_out, l0_ffn_norm, B, T)
    h, xn = conv_ffn(h, xn, l0_ffn_w_up, l0_ffn_conv_w, l0_ffn_conv_b, l0_ffn_w_down, l1_attn_norm, B, T)
    h, xn = na_mixer(h, xn, l1_na_w_qkv, l1_na_rpb, l1_na_w_out, l1_ffn_norm, B, T)
    h, xn = conv_ffn(h, xn, l1_ffn_w_up, l1_ffn_conv_w, l1_ffn_conv_b, l1_ffn_w_down, l2_attn_norm, B, T)
    h, xn = mla_mixer(h, xn, l2_mla_w_down, l2_mla_q_norm, l2_mla_kv_norm, l2_mla_w_uq, l2_mla_w_ukv, l2_mla_w_out,
                      l2_ffn_norm, B, T)
    h, xn = conv_ffn(h, xn, l2_ffn_w_up, l2_ffn_conv_w, l2_ffn_conv_b, l2_ffn_w_down, l3_attn_norm, B, T)
    h, xn = retention_mixer(h, xn, l3_ret_w_in, l3_ret_decay_fwd, l3_ret_decay_bwd, l3_ret_w_out, l3_ffn_norm, B, T)
    out = conv_ffn(h, xn, l3_ffn_w_up, l3_ffn_conv_w, l3_ffn_conv_b, l3_ffn_w_down, final_norm, B, T, final=True)
    return out.reshape(B, T, D)
```

```python
import functools

import jax
import jax.numpy as jnp
from jax import lax
from jax.experimental import pallas as pl
from jax.experimental.pallas import tpu as pltpu

F32 = jnp.float32
BF16 = jnp.bfloat16

GRID_W = 64
ROPE_BASE = 10000.0
NORM_EPS = 1e-6
GN_EPS = 1e-5

RET_HEADS = 8
RET_DK = 256
RET_DV = 512
RET_CHUNK = 256

NA_HEADS = 16
NA_DH = 128
NA_WIN_R = 8
NA_WIN_C = 16
NA_QROWS = 4
NA_KROWS = NA_QROWS + NA_WIN_R

MLA_HEADS = 16
MLA_Q_RANK = 512
MLA_KV_RANK = 512
MLA_NOPE = 128
MLA_ROPE = 64
MLA_V = 128
MLA_QK_PAD = 256
MLA_TQ = 512
MLA_TK = 512
LOG2_E = 1.4426950408889634

CONV_W = 3
MASK_NEG = -1e30
NORM_ROWS = 256

VMEM_LIMIT = 56 * 1024 * 1024


def _cparams(*sem):
    return pltpu.CompilerParams(dimension_semantics=sem, vmem_limit_bytes=VMEM_LIMIT)


def _dot(a, b):
    return jnp.dot(a, b, preferred_element_type=F32)


def _dot_nt(a, b):
    return lax.dot_general(a, b, (((1,), (1,)), ((), ())), preferred_element_type=F32)


def _dot_tn(a, b):
    return lax.dot_general(a, b, (((0,), (0,)), ((), ())), preferred_element_type=F32)


def _rms(x, g):
    return x * lax.rsqrt(jnp.mean(x * x, axis=-1, keepdims=True) + NORM_EPS) * g


def _pick(n, pref):
    if n <= pref:
        return n
    t = pref
    while n % t:
        t //= 2
    return t


def _rmsnorm_cast_body(x_ref, g_ref, o_ref):
    o_ref[...] = _rms(x_ref[...], g_ref[...]).astype(o_ref.dtype)


def rmsnorm_cast(x, g, tm=512):
    M, K = x.shape
    tm = _pick(M, tm)
    return pl.pallas_call(
        _rmsnorm_cast_body,
        grid=(M // tm,),
        in_specs=[pl.BlockSpec((tm, K), lambda i: (i, 0)), pl.BlockSpec((1, K), lambda i: (0, 0))],
        out_specs=pl.BlockSpec((tm, K), lambda i: (i, 0)),
        out_shape=jax.ShapeDtypeStruct((M, K), BF16),
        compiler_params=_cparams("parallel"),
        name="rmsnorm_cast",
    )(x, g.reshape(1, K))


def _matmul_res_body(a_ref, w_ref, r_ref, g_ref, o_ref, *xn_ref, final):
    k = pl.program_id(1)

    @pl.when(k == 0)
    def _():
        o_ref[...] = r_ref[...]

    last = pl.num_programs(1) - 1

    @pl.when(k < last)
    def _():
        o_ref[...] += _dot(a_ref[...], w_ref[...])

    @pl.when(k == last)
    def _():
        tm = o_ref.shape[0]
        rc = min(tm, NORM_ROWS)
        for r0 in range(0, tm, rc):
            rows = slice(r0, r0 + rc)
            h = o_ref[rows, :] + _dot(a_ref[rows, :], w_ref[...])
            y = _rms(h, g_ref[...])
            if final:
                o_ref[rows, :] = y
            else:
                o_ref[rows, :] = h
                xn_ref[0][rows, :] = y.astype(BF16)


def matmul_residual(a, w, r, gain, final=False, tm=1024, tk=512):
    M, K = a.shape
    N = w.shape[1]
    tm, tk = _pick(M, tm), _pick(K, tk)
    row_block = pl.BlockSpec((tm, N), lambda i, k: (i, 0))
    out_specs, out_shape = row_block, jax.ShapeDtypeStruct((M, N), F32)
    if not final:
        out_specs, out_shape = [row_block, row_block], [out_shape, jax.ShapeDtypeStruct((M, N), BF16)]
    return pl.pallas_call(
        functools.partial(_matmul_res_body, final=final),
        grid=(M // tm, K // tk),
        in_specs=[
            pl.BlockSpec((tm, tk), lambda i, k: (i, k)),
            pl.BlockSpec((tk, N), lambda i, k: (k, 0)),
            row_block,
            pl.BlockSpec((1, N), lambda i, k: (0, 0)),
        ],
        out_specs=out_specs,
        out_shape=out_shape,
        compiler_params=_cparams("parallel", "arbitrary"),
        name="matmul_residual",
    )(a, w, r, gain.reshape(1, N))


def _matmul_body(a_ref, w_ref, cs_ref, o_ref):
    o_ref[...] = (_dot(a_ref[...], w_ref[...]) * cs_ref[...]).astype(o_ref.dtype)


def matmul(a, w, col_scale=None, tm=1024, tn=2048):
    M, K = a.shape
    N = w.shape[1]
    tm, tn = _pick(M, tm), _pick(N, tn)
    if col_scale is None:
        col_scale = jnp.ones((N,), F32)
    return pl.pallas_call(
        _matmul_body,
        grid=(M // tm, N // tn),
        in_specs=[
            pl.BlockSpec((tm, K), lambda i, j: (i, 0)),
            pl.BlockSpec((K, tn), lambda i, j: (0, j)),
            pl.BlockSpec((1, tn), lambda i, j: (0, j)),
        ],
        out_specs=pl.BlockSpec((tm, tn), lambda i, j: (i, j)),
        out_shape=jax.ShapeDtypeStruct((M, N), BF16),
        compiler_params=_cparams("parallel", "parallel"),
        name="matmul",
    )(a, w, col_scale.reshape(1, N))


def _rope_tables(T, d):
    half = d // 2
    inv = ROPE_BASE ** (-jnp.arange(half, dtype=F32) * 2.0 / d)
    ang = jnp.arange(T, dtype=F32)[:, None] * inv[None, :]
    return jnp.cos(ang), jnp.sin(ang)


def _ret_in_body(x_ref, w_ref, cos_ref, sin_ref, o_ref, *, n_q_tiles):
    j = pl.program_id(1)

    @pl.when(j < 2 * n_q_tiles)
    def _():
        cos = cos_ref[...]
        sin = sin_ref[...]
        scale = jnp.where(j >= n_q_tiles, RET_DK ** -0.5, 1.0).astype(F32)
        half = RET_DK // 2
        for c0 in range(0, o_ref.shape[1], RET_DK):
            acc = _dot(x_ref[...], w_ref[:, c0:c0 + RET_DK])
            x1 = acc[:, :half]
            x2 = acc[:, half:]
            o_ref[:, c0:c0 + half] = ((x1 * cos - x2 * sin) * scale).astype(o_ref.dtype)
            o_ref[:, c0 + half:c0 + RET_DK] = ((x1 * sin + x2 * cos) * scale).astype(o_ref.dtype)

    @pl.when(j >= 2 * n_q_tiles)
    def _():
        o_ref[...] = _dot(x_ref[...], w_ref[...]).astype(o_ref.dtype)


def ret_in_proj(x, w, T, tm=1024, tn=2048):
    M, K = x.shape
    N = w.shape[1]
    tm, tn = _pick(T, tm), _pick(N, tn)
    nT = T // tm
    qk_w = RET_HEADS * RET_DK
    assert tn % RET_DK == 0 and qk_w % tn == 0
    cos, sin = _rope_tables(T, RET_DK)
    return pl.pallas_call(
        functools.partial(_ret_in_body, n_q_tiles=qk_w // tn),
        grid=(M // tm, N // tn),
        in_specs=[
            pl.BlockSpec((tm, K), lambda i, j: (i, 0)),
            pl.BlockSpec((K, tn), lambda i, j: (0, j)),
            pl.BlockSpec((tm, RET_DK // 2), lambda i, j: (i % nT, 0)),
            pl.BlockSpec((tm, RET_DK // 2), lambda i, j: (i % nT, 0)),
        ],
        out_specs=pl.BlockSpec((tm, tn), lambda i, j: (i, j)),
        out_shape=jax.ShapeDtypeStruct((M, N), BF16),
        compiler_params=_cparams("parallel", "parallel"),
        name="ret_in_proj",
    )(x, w, cos, sin)


def _retention_body(lgf_ref, lgb_ref, q_ref, k_ref, v_ref, g_ref, y_ref,
                    decay_ref, sf_all_ref, sf_ref, sb_ref, *, C):
    h = pl.program_id(1)
    T = q_ref.shape[0]
    N = T // C
    lg_f = lgf_ref[h]
    lg_b = lgb_ref[h]

    idx = lax.broadcasted_iota(jnp.int32, (C, 1), 0).astype(F32)
    kd_f = jnp.exp((C - 1.0 - idx) * lg_f)
    kd_b = jnp.exp(idx * lg_b)
    qd_f = jnp.exp((idx + 1.0) * lg_f)
    qd_b = jnp.exp((C - idx) * lg_b)
    chunk_len = jnp.full((1, 1), C, F32)
    cd_f = jnp.exp(chunk_len * lg_f)
    cd_b = jnp.exp(chunk_len * lg_b)

    row = lax.broadcasted_iota(jnp.int32, (C, C), 0)
    col = lax.broadcasted_iota(jnp.int32, (C, C), 1)
    decay_ref[...] = jnp.exp(jnp.where(row >= col, (row - col).astype(F32) * lg_f,
                                       (col - row).astype(F32) * lg_b))

    def rows(n):
        return pl.ds(pl.multiple_of(n * C, C), C)

    sf_ref[...] = jnp.zeros_like(sf_ref)
    sb_ref[...] = jnp.zeros_like(sb_ref)

    def forward_state(n, carry):
        k = k_ref[rows(n), :].astype(F32)
        sf_all_ref[n] = sf_ref[...].astype(BF16)
        sf_ref[...] = sf_ref[...] * cd_f + _dot_tn((k * kd_f).astype(BF16), v_ref[rows(n), :])
        return carry

    lax.fori_loop(0, N, forward_state, 0)

    def output_chunk(i, carry):
        n = N - 1 - i
        q = q_ref[rows(n), :]
        k = k_ref[rows(n), :]
        v = v_ref[rows(n), :]
        qf = q.astype(F32)
        scores = (_dot_nt(q, k) * decay_ref[...]).astype(BF16)
        o = (_dot(scores, v)
             + _dot((qf * qd_f).astype(BF16), sf_all_ref[n])
             + _dot((qf * qd_b).astype(BF16), sb_ref[...].astype(BF16)))
        mu = jnp.mean(o, axis=-1, keepdims=True)
        d = o - mu
        var = jnp.mean(d * d, axis=-1, keepdims=True)
        o = d * lax.rsqrt(var + GN_EPS)
        g = g_ref[rows(n), :].astype(F32)
        y_ref[rows(n), :] = (g * jax.nn.sigmoid(g) * o).astype(y_ref.dtype)
        sb_ref[...] = sb_ref[...] * cd_b + _dot_tn((k.astype(F32) * kd_b).astype(BF16), v)
        return carry

    lax.fori_loop(0, N, output_chunk, 0, unroll=4)


def retention_core(proj, lg_f, lg_b, B, T):
    H, dk, dv = RET_HEADS, RET_DK, RET_DV
    C = min(RET_CHUNK, T)
    N = T // C
    v_blk0 = 2 * H * dk // dv
    g_blk0 = v_blk0 + H
    smem = pl.BlockSpec(memory_space=pltpu.SMEM)
    return pl.pallas_call(
        functools.partial(_retention_body, C=C),
        grid=(B, H),
        in_specs=[
            smem, smem,
            pl.BlockSpec((T, dk), lambda b, h: (b, h)),
            pl.BlockSpec((T, dk), lambda b, h: (b, H + h)),
            pl.BlockSpec((T, dv), lambda b, h: (b, v_blk0 + h)),
            pl.BlockSpec((T, dv), lambda b, h: (b, g_blk0 + h)),
        ],
        out_specs=pl.BlockSpec((T, dv), lambda b, h: (b, h)),
        out_shape=jax.ShapeDtypeStruct((B * T, H * dv), BF16),
        scratch_shapes=[pltpu.VMEM((C, C), F32), pltpu.VMEM((N, dk, dv), BF16),
                        pltpu.VMEM((dk, dv), F32), pltpu.VMEM((dk, dv), F32)],
        compiler_params=_cparams("parallel", "parallel"),
        name="retention",
    )(lg_f, lg_b, proj, proj, proj, proj)


def retention_mixer(h, xn, w_in, decay_fwd, decay_bwd, w_out, next_gain, B, T):
    proj = ret_in_proj(xn, w_in.astype(BF16), T)
    lg_f = jax.nn.log_sigmoid(decay_fwd.astype(F32))
    lg_b = jax.nn.log_sigmoid(decay_bwd.astype(F32))
    y = retention_core(proj, lg_f, lg_b, B, T)
    return matmul_residual(y, w_out.astype(BF16), h, next_gain)


def _na_block_kinds(rows):
    return ((0, 0), (NA_QROWS, 0), (rows - NA_QROWS, rows - NA_KROWS))


def _na_row_pair(rows, r0, ks, a, bb):
    rq, rk = r0 + a, ks + bb
    r_start = min(max(rq - NA_WIN_R // 2, 0), rows - NA_WIN_R)
    return r_start <= rk < r_start + NA_WIN_R, rk - rq + NA_WIN_R - 1


def _na_build_bias(rpb_ref, tile_ref, bias_ref, h, rows):
    W, R, KR = GRID_W, NA_QROWS, NA_KROWS
    n_dr, n_dc = 2 * NA_WIN_R - 1, 2 * NA_WIN_C - 1
    lane = lax.broadcasted_iota(jnp.int32, (W, 2 * W), 1)
    cq = lax.broadcasted_iota(jnp.int32, (W, 2 * W), 0)
    ck = lane & (W - 1)
    hi = lane >= W
    dc = ck - cq + NA_WIN_C - 1
    c_start = jnp.clip(cq - NA_WIN_C // 2, 0, W - NA_WIN_C)
    col_ok = (ck >= c_start) & (ck < c_start + NA_WIN_C)
    base = h * (n_dr * n_dc)
    for t in range(n_dr):
        acc = jnp.zeros((W, 2 * W), F32)
        for j in range(n_dc):
            lo = rpb_ref[base + t * n_dc + j]
            hv = rpb_ref[base + (t + 1) * n_dc + j] if t + 1 < n_dr else 0.0
            acc = jnp.where(dc == j, jnp.where(hi, hv, lo), acc)
        tile_ref[t] = jnp.where(col_ok, acc, MASK_NEG)
    for c, (r0, ks) in enumerate(_na_block_kinds(rows)):
        for a in range(R):
            for pp in range(KR // 2):
                ok0, dr0 = _na_row_pair(rows, r0, ks, a, 2 * pp)
                ok1, dr1 = _na_row_pair(rows, r0, ks, a, 2 * pp + 1)
                if ok0 or ok1:
                    assert 0 <= dr0 < n_dr and (dr1 < n_dr or not ok1)
                    tile = tile_ref[dr0]
                    if not ok0:
                        tile = jnp.where(hi, tile, MASK_NEG)
                    if not ok1:
                        tile = jnp.where(hi, MASK_NEG, tile)
                else:
                    tile = jnp.full((W, 2 * W), MASK_NEG, F32)
                bias_ref[c, a * W:(a + 1) * W, pp * 2 * W:(pp + 1) * 2 * W] = tile


def _na_body(rpb_ref, q_ref, k_ref, v_ref, o_ref, tile_ref, bias_ref, *, rows):
    W, R, KR = GRID_W, NA_QROWS, NA_KROWS
    nb = rows // R
    tq, tk = R * W, KR * W

    @pl.when(pl.program_id(1) == 0)
    def _():
        _na_build_bias(rpb_ref, tile_ref, bias_ref, pl.program_id(0), rows)

    def key_rows(i):
        k0 = min(max(R * i - R, 0), rows - KR) * W
        return slice(k0, k0 + tk)

    def scores(i):
        kind = 0 if i == 0 else (2 if i == nb - 1 else 1)
        return _dot_nt(q_ref[i * tq:(i + 1) * tq, :], k_ref[key_rows(i), :]) + bias_ref[kind]

    s = scores(0)
    for i in range(nb):
        s_next = scores(i + 1) if i + 1 < nb else None
        m = jnp.max(s, axis=-1, keepdims=True)
        p = jnp.exp(s - m)
        l = jnp.sum(p, axis=-1, keepdims=True)
        o = _dot(p.astype(BF16), v_ref[key_rows(i), :]) / l
        o_ref[i * tq:(i + 1) * tq, :] = o.astype(o_ref.dtype)
        s = s_next


def na_core(qkv, rpb, B, T):
    H, dh, W = NA_HEADS, NA_DH, GRID_W
    rows = T // W
    tq, tk = NA_QROWS * W, NA_KROWS * W
    return pl.pallas_call(
        functools.partial(_na_body, rows=rows),
        grid=(H, B),
        in_specs=[
            pl.BlockSpec(memory_space=pltpu.SMEM),
            pl.BlockSpec((T, dh), lambda h, b: (b, h)),
            pl.BlockSpec((T, dh), lambda h, b: (b, H + h)),
            pl.BlockSpec((T, dh), lambda h, b: (b, 2 * H + h)),
        ],
        out_specs=pl.BlockSpec((T, dh), lambda h, b: (b, h)),
        out_shape=jax.ShapeDtypeStruct((B * T, H * dh), BF16),
        scratch_shapes=[pltpu.VMEM((2 * NA_WIN_R - 1, W, 2 * W), F32), pltpu.VMEM((3, tq, tk), F32)],
        compiler_params=_cparams("parallel", "arbitrary"),
        name="neighbourhood_attention",
    )(rpb.astype(F32).reshape(-1), qkv, qkv, qkv)


def na_mixer(h, xn, w_qkv, rpb, w_out, next_gain, B, T):
    H, dh = NA_HEADS, NA_DH
    rows = T // GRID_W
    assert rows % NA_QROWS == 0 and rows >= NA_KROWS and NA_KROWS % 2 == 0
    col_scale = jnp.concatenate([jnp.full((H * dh,), dh ** -0.5, F32), jnp.ones((2 * H * dh,), F32)])
    qkv = matmul(xn, w_qkv.astype(BF16), col_scale)
    o = na_core(qkv, rpb, B, T)
    return matmul_residual(o, w_out.astype(BF16), h, next_gain)


def _mla_down_body(x_ref, wq_ref, wkv_ref, wr_ref, wrs_ref, qn_ref, kvn_ref,
                   cos_ref, sin_ref, cq_ref, ckv_ref, kr_ref):
    xn = x_ref[...]
    cq_ref[...] = _rms(_dot(xn, wq_ref[...]), qn_ref[...]).astype(cq_ref.dtype)
    ckv_ref[...] = _rms(_dot(xn, wkv_ref[...]), kvn_ref[...]).astype(ckv_ref.dtype)
    kr = _dot(xn, wr_ref[...]) * cos_ref[...] + _dot(xn, wrs_ref[...]) * sin_ref[...]
    kr_ref[...] = kr.astype(kr_ref.dtype)


def _swap_halves(w, group):
    K, N = w.shape
    w = w.reshape(K, N // group, 2, group // 2)
    return w[:, :, ::-1, :].reshape(K, N)


def mla_down(xn, w_down, q_norm, kv_norm, T, tm=1024):
    M, D = xn.shape
    tm = _pick(T, tm)
    nT = T // tm
    w = w_down.astype(BF16)
    qr, kvr, rp = MLA_Q_RANK, MLA_KV_RANK, MLA_ROPE
    w_q, w_kv, w_r = w[:, :qr], w[:, qr:qr + kvr], w[:, qr + kvr:]
    pad = jnp.zeros((D, 128 - rp), BF16)
    w_rs = jnp.concatenate([_swap_halves(w_r, rp), pad], axis=1)
    w_r = jnp.concatenate([w_r, pad], axis=1)
    cos, sin = _rope_tables(T, rp)
    zpad = jnp.zeros((T, 128 - rp), F32)
    cos_t = jnp.concatenate([cos, cos, zpad], axis=1)
    sin_t = jnp.concatenate([-sin, sin, zpad], axis=1)
    full = lambda shape: pl.BlockSpec(shape, lambda i: (0, 0))
    return pl.pallas_call(
        _mla_down_body,
        grid=(M // tm,),
        in_specs=[
            pl.BlockSpec((tm, D), lambda i: (i, 0)),
            full((D, qr)), full((D, kvr)), full((D, 128)), full((D, 128)),
            full((1, qr)), full((1, kvr)),
            pl.BlockSpec((tm, 128), lambda i: (i % nT, 0)),
            pl.BlockSpec((tm, 128), lambda i: (i % nT, 0)),
        ],
        out_specs=[
            pl.BlockSpec((tm, qr), lambda i: (i, 0)),
            pl.BlockSpec((tm, kvr), lambda i: (i, 0)),
            pl.BlockSpec((tm, 128), lambda i: (i, 0)),
        ],
        out_shape=[
            jax.ShapeDtypeStruct((M, qr), BF16),
            jax.ShapeDtypeStruct((M, kvr), BF16),
            jax.ShapeDtypeStruct((M, 128), BF16),
        ],
        compiler_params=_cparams("parallel"),
        name="mla_down",
    )(xn, w_q, w_kv, w_r, w_rs, q_norm.reshape(1, qr), kv_norm.reshape(1, kvr), cos_t, sin_t)


def _mla_q_body(c_ref, w_ref, ws_ref, cos_ref, sin_ref, o_ref):
    c = c_ref[...]
    P = MLA_QK_PAD
    scale = (MLA_NOPE + MLA_ROPE) ** -0.5 * LOG2_E
    cos = cos_ref[...]
    sin = sin_ref[...]
    for j in range(o_ref.shape[1] // P):
        sl = slice(j * P, (j + 1) * P)
        x = _dot(c, w_ref[:, sl])
        xs = _dot(c, ws_ref[:, sl])
        o_ref[:, sl] = ((x * cos + xs * sin) * scale).astype(o_ref.dtype)


def mla_q(c_q, w_uq, T, tm=1024, heads_per_step=4):
    M, R = c_q.shape
    H, P, nope, rp = MLA_HEADS, MLA_QK_PAD, MLA_NOPE, MLA_ROPE
    tm = _pick(T, tm)
    nT = T // tm
    w = w_uq.astype(BF16).reshape(R, H, nope + rp)
    zeros = lambda n: jnp.zeros((R, H, n), BF16)
    w_rope = w[:, :, nope:]
    w_main = jnp.concatenate([w[:, :, :nope], w_rope, zeros(P - nope - rp)], axis=2).reshape(R, H * P)
    w_rope_sw = _swap_halves(w_rope.reshape(R, H * rp), rp).reshape(R, H, rp)
    w_swap = jnp.concatenate([zeros(nope), w_rope_sw, zeros(P - nope - rp)], axis=2).reshape(R, H * P)
    cos, sin = _rope_tables(T, rp)
    cos_t = jnp.concatenate([jnp.ones((T, nope), F32), cos, cos, jnp.zeros((T, P - nope - rp), F32)], axis=1)
    sin_t = jnp.concatenate([jnp.zeros((T, nope), F32), -sin, sin, jnp.zeros((T, P - nope - rp), F32)], axis=1)
    tn = heads_per_step * P
    return pl.pallas_call(
        _mla_q_body,
        grid=(M // tm, H * P // tn),
        in_specs=[
            pl.BlockSpec((tm, R), lambda i, j: (i, 0)),
            pl.BlockSpec((R, tn), lambda i, j: (0, j)),
            pl.BlockSpec((R, tn), lambda i, j: (0, j)),
            pl.BlockSpec((tm, P), lambda i, j: (i % nT, 0)),
            pl.BlockSpec((tm, P), lambda i, j: (i % nT, 0)),
        ],
        out_specs=pl.BlockSpec((tm, tn), lambda i, j: (i, j)),
        out_shape=jax.ShapeDtypeStruct((M, H * P), BF16),
        compiler_params=_cparams("parallel", "parallel"),
        name="mla_q",
    )(c_q, w_main, w_swap, cos_t, sin_t)


def _mla_attn_body(q_ref, kn_ref, kr_ref, v_ref, o_ref, kcat_ref, *, tk):
    T = kn_ref.shape[0]

    @pl.when(pl.program_id(2) == 0)
    def _():
        kcat_ref[:, :MLA_NOPE] = kn_ref[...]
        kcat_ref[:, MLA_NOPE:] = kr_ref[...]

    q = q_ref[...]
    tq = q.shape[0]
    nk = T // tk

    def scores(j):
        return _dot_nt(q, kcat_ref[j * tk:(j + 1) * tk, :])

    m = jnp.full((tq, 1), -jnp.inf, F32)
    l = jnp.zeros((tq, 1), F32)
    acc = jnp.zeros((tq, MLA_V), F32)
    s = scores(0)
    for j in range(nk):
        s_next = scores(j + 1) if j + 1 < nk else None
        m_new = jnp.maximum(m, jnp.max(s, axis=-1, keepdims=True))
        alpha = jnp.exp2(m - m_new)
        p = jnp.exp2(s - m_new)
        l = alpha * l + jnp.sum(p, axis=-1, keepdims=True)
        acc = alpha * acc + _dot(p.astype(BF16), v_ref[j * tk:(j + 1) * tk, :])
        m, s = m_new, s_next
    o_ref[...] = (acc / l).astype(o_ref.dtype)


def mla_attention(q, kv, k_rope, B, T):
    H, P = MLA_HEADS, MLA_QK_PAD
    tq, tk = _pick(T, MLA_TQ), _pick(T, MLA_TK)
    nq = T // tq
    return pl.pallas_call(
        functools.partial(_mla_attn_body, tk=tk),
        grid=(B, H, nq),
        in_specs=[
            pl.BlockSpec((tq, P), lambda b, h, i: (b * nq + i, h)),
            pl.BlockSpec((T, MLA_NOPE), lambda b, h, i: (b, h)),
            pl.BlockSpec((T, 128), lambda b, h, i: (b, 0)),
            pl.BlockSpec((T, MLA_V), lambda b, h, i: (b, H + h)),
        ],
        out_specs=pl.BlockSpec((tq, MLA_V), lambda b, h, i: (b * nq + i, h)),
        out_shape=jax.ShapeDtypeStruct((B * T, H * MLA_V), BF16),
        scratch_shapes=[pltpu.VMEM((T, P), BF16)],
        compiler_params=_cparams("parallel", "parallel", "arbitrary"),
        name="mla_attention",
    )(q, kv, k_rope, kv)


def mla_mixer(h, xn, w_down, q_norm, kv_norm, w_uq, w_ukv, w_out, next_gain, B, T):
    H = MLA_HEADS
    c_q, c_kv, k_rope = mla_down(xn, w_down, q_norm, kv_norm, T)
    q = mla_q(c_q, w_uq, T)
    w = w_ukv.astype(BF16).reshape(MLA_KV_RANK, H, MLA_NOPE + MLA_V)
    w_kv = jnp.concatenate([w[:, :, :MLA_NOPE].reshape(MLA_KV_RANK, H * MLA_NOPE),
                            w[:, :, MLA_NOPE:].reshape(MLA_KV_RANK, H * MLA_V)], axis=1)
    kv = matmul(c_kv, w_kv)
    o = mla_attention(q, kv, k_rope, B, T)
    return matmul_residual(o, w_out.astype(BF16), h, next_gain)


def _ffn_up_body(x_ref, wg_ref, wv_ref, cwg_ref, cwv_ref, bg_ref, bv_ref, o_ref, ug_ref, uv_ref, *, rc):
    T, tn = o_ref.shape
    n = T // rc
    first = lax.broadcasted_iota(jnp.int32, (rc, 1), 0) == 0
    last = lax.broadcasted_iota(jnp.int32, (rc, 1), 0) == rc - 1
    zero = jnp.zeros((1, tn), F32)

    def project(c):
        x = x_ref[c * rc:(c + 1) * rc, :]
        ug_ref[c % 2] = _dot(x, wg_ref[...])
        uv_ref[c % 2] = _dot(x, wv_ref[...])

    def conv(u, before, after, w_ref, b_ref):
        prev = jnp.where(first, before, pltpu.roll(u, 1, axis=0))
        nxt = jnp.where(last, after, pltpu.roll(u, rc - 1, axis=0))
        return prev * w_ref[0:1, :] + u * w_ref[1:2, :] + nxt * w_ref[2:3, :] + b_ref[...]

    project(0)
    before = (zero, zero)
    for c in range(n):
        if c + 1 < n:
            project(c + 1)
            after = (ug_ref[(c + 1) % 2, 0:1, :], uv_ref[(c + 1) % 2, 0:1, :])
        else:
            after = (zero, zero)
        ug = ug_ref[c % 2]
        uv = uv_ref[c % 2]
        g = conv(ug, before[0], after[0], cwg_ref, bg_ref)
        val = conv(uv, before[1], after[1], cwv_ref, bv_ref)
        o_ref[c * rc:(c + 1) * rc, :] = (g * jax.nn.sigmoid(g) * val).astype(o_ref.dtype)
        before = (ug[rc - 1:rc, :], uv[rc - 1:rc, :])


def ffn_up_conv_gate(xn, w_up, conv_w, conv_b, B, T, tn=512, rc=512):
    D = xn.shape[1]
    F = w_up.shape[1] // 2
    tn, rc = _pick(F, tn), _pick(T, rc)
    nj = F // tn
    b2 = conv_b.reshape(1, 2 * F).astype(F32)
    cw = conv_w.astype(F32)
    return pl.pallas_call(
        functools.partial(_ffn_up_body, rc=rc),
        grid=(B, nj),
        in_specs=[
            pl.BlockSpec((T, D), lambda b, j: (b, 0), pipeline_mode=pl.Buffered(1)),
            pl.BlockSpec((D, tn), lambda b, j: (0, j)),
            pl.BlockSpec((D, tn), lambda b, j: (0, nj + j)),
            pl.BlockSpec((CONV_W, tn), lambda b, j: (0, j)),
            pl.BlockSpec((CONV_W, tn), lambda b, j: (0, nj + j)),
            pl.BlockSpec((1, tn), lambda b, j: (0, j)),
            pl.BlockSpec((1, tn), lambda b, j: (0, nj + j)),
        ],
        out_specs=pl.BlockSpec((T, tn), lambda b, j: (b, j)),
        out_shape=jax.ShapeDtypeStruct((B * T, F), BF16),
        scratch_shapes=[pltpu.VMEM((2, rc, tn), F32), pltpu.VMEM((2, rc, tn), F32)],
        compiler_params=_cparams("parallel", "arbitrary"),
        name="ffn_up_conv_gate",
    )(xn, w_up, w_up, cw, cw, b2, b2)


def conv_ffn(h, xn, w_up, conv_w, conv_b, w_down, next_gain, B, T, final=False):
    a = ffn_up_conv_gate(xn, w_up.astype(BF16), conv_w, conv_b, B, T)
    return matmul_residual(a, w_down.astype(BF16), h, next_gain, final=final)


def kernel(x, l0_attn_norm, l0_ret_w_in, l0_ret_decay_fwd, l0_ret_decay_bwd, l0_ret_w_out, l0_ffn_norm, l0_ffn_w_up, l0_ffn_conv_w, l0_ffn_conv_b, l0_ffn_w_down, l1_attn_norm, l1_na_w_qkv, l1_na_rpb, l1_na_w_out, l1_ffn_norm, l1_ffn_w_up, l1_ffn_conv_w, l1_ffn_conv_b, l1_ffn_w_down, l2_attn_norm, l2_mla_w_down, l2_mla_q_norm, l2_mla_kv_norm, l2_mla_w_uq, l2_mla_w_ukv, l2_mla_w_out, l2_ffn_norm, l2_ffn_w_up, l2_ffn_conv_w, l2_ffn_conv_b, l2_ffn_w_down, l3_attn_norm, l3_ret_w_in, l3_ret_decay_fwd, l3_ret_decay_bwd, l3_ret_w_out, l3_ffn_norm, l3_ffn_w_up, l3_ffn_conv_w, l3_ffn_conv_b, l3_ffn_w_down, final_norm):
    B, T, D = x.shape
    h = x.reshape(B * T, D)
    xn = rmsnorm_cast(h, l0_attn_norm)
    h, xn = retention_mixer(h, xn, l0_ret_w_in, l0_ret_decay_fwd, l0_ret_decay_bwd, l0_ret_w_out, l0_ffn_norm, B, T)
    h, xn = conv_ffn(h, xn, l0_ffn_w_up, l0_ffn_conv_w, l0_ffn_conv_b, l0_ffn_w_down, l1_attn_norm, B, T)
    h, xn = na_mixer(h, xn, l1_na_w_qkv, l1_na_rpb, l1_na_w_out, l1_ffn_norm, B, T)
    h, xn = conv_ffn(h, xn, l1_ffn_w_up, l1_ffn_conv_w, l1_ffn_conv_b, l1_ffn_w_down, l2_attn_norm, B, T)
    h, xn = mla_mixer(h, xn, l2_mla_w_down, l2_mla_q_norm, l2_mla_kv_norm, l2_mla_w_uq, l2_mla_w_ukv, l2_mla_w_out,
                      l2_ffn_norm, B, T)
    h, xn = conv_ffn(h, xn, l2_ffn_w_up, l2_ffn_conv_w, l2_ffn_conv_b, l2_ffn_w_down, l3_attn_norm, B, T)
    h, xn = retention_mixer(h, xn, l3_ret_w_in, l3_ret_decay_fwd, l3_ret_decay_bwd, l3_ret_w_out, l3_ffn_norm, B, T)
    out = conv_ffn(h, xn, l3_ffn_w_up, l3_ffn_conv_w, l3_ffn_conv_b, l3_ffn_w_down, final_norm, B, T, final=True)
    return out.reshape(B, T, D)
```

```python
import functools

import jax
import jax.numpy as jnp
from jax import lax
from jax.experimental import pallas as pl
from jax.experimental.pallas import tpu as pltpu

F32 = jnp.float32
BF16 = jnp.bfloat16

GRID_W = 64
ROPE_BASE = 10000.0
NORM_EPS = 1e-6
GN_EPS = 1e-5

RET_HEADS = 8
RET_DK = 256
RET_DV = 512
RET_CHUNK = 256

NA_HEADS = 16
NA_DH = 128
NA_WIN_R = 8
NA_WIN_C = 16
NA_QROWS = 4
NA_KROWS = NA_QROWS + NA_WIN_R

MLA_HEADS = 16
MLA_Q_RANK = 512
MLA_KV_RANK = 512
MLA_NOPE = 128
MLA_ROPE = 64
MLA_V = 128
MLA_QK_PAD = 256
MLA_TQ = 1024
MLA_TK = 1024
LOG2_E = 1.4426950408889634

CONV_W = 3
MASK_NEG = -1e30
NORM_ROWS = 256

VMEM_LIMIT = 56 * 1024 * 1024


def _cparams(*sem):
    return pltpu.CompilerParams(dimension_semantics=sem, vmem_limit_bytes=VMEM_LIMIT)


def _dot(a, b):
    return jnp.dot(a, b, preferred_element_type=F32)


def _dot_nt(a, b):
    return lax.dot_general(a, b, (((1,), (1,)), ((), ())), preferred_element_type=F32)


def _dot_tn(a, b):
    return lax.dot_general(a, b, (((0,), (0,)), ((), ())), preferred_element_type=F32)


def _rms(x, g):
    return x * lax.rsqrt(jnp.mean(x * x, axis=-1, keepdims=True) + NORM_EPS) * g


def _pick(n, pref):
    if n <= pref:
        return n
    t = pref
    while n % t:
        t //= 2
    return t


def _rmsnorm_cast_body(x_ref, g_ref, o_ref):
    o_ref[...] = _rms(x_ref[...], g_ref[...]).astype(o_ref.dtype)


def rmsnorm_cast(x, g, tm=512):
    M, K = x.shape
    tm = _pick(M, tm)
    return pl.pallas_call(
        _rmsnorm_cast_body,
        grid=(M // tm,),
        in_specs=[pl.BlockSpec((tm, K), lambda i: (i, 0)), pl.BlockSpec((1, K), lambda i: (0, 0))],
        out_specs=pl.BlockSpec((tm, K), lambda i: (i, 0)),
        out_shape=jax.ShapeDtypeStruct((M, K), BF16),
        compiler_params=_cparams("parallel"),
        name="rmsnorm_cast",
    )(x, g.reshape(1, K))


def _matmul_res_body(a_ref, w_ref, r_ref, g_ref, o_ref, *xn_ref, final):
    k = pl.program_id(1)

    @pl.when(k == 0)
    def _():
        o_ref[...] = r_ref[...]

    last = pl.num_programs(1) - 1

    @pl.when(k < last)
    def _():
        o_ref[...] += _dot(a_ref[...], w_ref[...])

    @pl.when(k == last)
    def _():
        tm = o_ref.shape[0]
        rc = min(tm, NORM_ROWS)
        for r0 in range(0, tm, rc):
            rows = slice(r0, r0 + rc)
            h = o_ref[rows, :] + _dot(a_ref[rows, :], w_ref[...])
            y = _rms(h, g_ref[...])
            if final:
                o_ref[rows, :] = y
            else:
                o_ref[rows, :] = h
                xn_ref[0][rows, :] = y.astype(BF16)


def matmul_residual(a, w, r, gain, final=False, tm=1024, tk=1024):
    M, K = a.shape
    N = w.shape[1]
    tm, tk = _pick(M, tm), _pick(K, tk)
    row_block = pl.BlockSpec((tm, N), lambda i, k: (i, 0))
    out_specs, out_shape = row_block, jax.ShapeDtypeStruct((M, N), F32)
    if not final:
        out_specs, out_shape = [row_block, row_block], [out_shape, jax.ShapeDtypeStruct((M, N), BF16)]
    return pl.pallas_call(
        functools.partial(_matmul_res_body, final=final),
        grid=(M // tm, K // tk),
        in_specs=[
            pl.BlockSpec((tm, tk), lambda i, k: (i, k)),
            pl.BlockSpec((tk, N), lambda i, k: (k, 0)),
            row_block,
            pl.BlockSpec((1, N), lambda i, k: (0, 0)),
        ],
        out_specs=out_specs,
        out_shape=out_shape,
        compiler_params=_cparams("parallel", "arbitrary"),
        name="matmul_residual",
    )(a, w, r, gain.reshape(1, N))


def _matmul_body(a_ref, w_ref, cs_ref, o_ref):
    o_ref[...] = (_dot(a_ref[...], w_ref[...]) * cs_ref[...]).astype(o_ref.dtype)


def matmul(a, w, col_scale=None, tm=1024, tn=2048):
    M, K = a.shape
    N = w.shape[1]
    tm, tn = _pick(M, tm), _pick(N, tn)
    if col_scale is None:
        col_scale = jnp.ones((N,), F32)
    return pl.pallas_call(
        _matmul_body,
        grid=(M // tm, N // tn),
        in_specs=[
            pl.BlockSpec((tm, K), lambda i, j: (i, 0)),
            pl.BlockSpec((K, tn), lambda i, j: (0, j)),
            pl.BlockSpec((1, tn), lambda i, j: (0, j)),
        ],
        out_specs=pl.BlockSpec((tm, tn), lambda i, j: (i, j)),
        out_shape=jax.ShapeDtypeStruct((M, N), BF16),
        compiler_params=_cparams("parallel", "parallel"),
        name="matmul",
    )(a, w, col_scale.reshape(1, N))


def _rope_tables(T, d):
    half = d // 2
    inv = ROPE_BASE ** (-jnp.arange(half, dtype=F32) * 2.0 / d)
    ang = jnp.arange(T, dtype=F32)[:, None] * inv[None, :]
    return jnp.cos(ang), jnp.sin(ang)


def _ret_in_body(x_ref, w_ref, cos_ref, sin_ref, o_ref, *, n_q_tiles):
    j = pl.program_id(1)

    @pl.when(j < 2 * n_q_tiles)
    def _():
        cos = cos_ref[...]
        sin = sin_ref[...]
        scale = jnp.where(j >= n_q_tiles, RET_DK ** -0.5, 1.0).astype(F32)
        half = RET_DK // 2
        for c0 in range(0, o_ref.shape[1], RET_DK):
            acc = _dot(x_ref[...], w_ref[:, c0:c0 + RET_DK])
            x1 = acc[:, :half]
            x2 = acc[:, half:]
            o_ref[:, c0:c0 + half] = ((x1 * cos - x2 * sin) * scale).astype(o_ref.dtype)
            o_ref[:, c0 + half:c0 + RET_DK] = ((x1 * sin + x2 * cos) * scale).astype(o_ref.dtype)

    @pl.when(j >= 2 * n_q_tiles)
    def _():
        o_ref[...] = _dot(x_ref[...], w_ref[...]).astype(o_ref.dtype)


def ret_in_proj(x, w, T, tm=1024, tn=2048):
    M, K = x.shape
    N = w.shape[1]
    tm, tn = _pick(T, tm), _pick(N, tn)
    nT = T // tm
    qk_w = RET_HEADS * RET_DK
    assert tn % RET_DK == 0 and qk_w % tn == 0
    cos, sin = _rope_tables(T, RET_DK)
    return pl.pallas_call(
        functools.partial(_ret_in_body, n_q_tiles=qk_w // tn),
        grid=(M // tm, N // tn),
        in_specs=[
            pl.BlockSpec((tm, K), lambda i, j: (i, 0)),
            pl.BlockSpec((K, tn), lambda i, j: (0, j)),
            pl.BlockSpec((tm, RET_DK // 2), lambda i, j: (i % nT, 0)),
            pl.BlockSpec((tm, RET_DK // 2), lambda i, j: (i % nT, 0)),
        ],
        out_specs=pl.BlockSpec((tm, tn), lambda i, j: (i, j)),
        out_shape=jax.ShapeDtypeStruct((M, N), BF16),
        compiler_params=_cparams("parallel", "parallel"),
        name="ret_in_proj",
    )(x, w, cos, sin)


def _retention_body(lgf_ref, lgb_ref, q_ref, k_ref, v_ref, g_ref, y_ref,
                    decay_ref, sf_all_ref, sf_ref, sb_ref, *, C):
    h = pl.program_id(1)
    T = q_ref.shape[0]
    N = T // C
    lg_f = lgf_ref[h]
    lg_b = lgb_ref[h]

    idx = lax.broadcasted_iota(jnp.int32, (C, 1), 0).astype(F32)
    kd_f = jnp.exp((C - 1.0 - idx) * lg_f)
    kd_b = jnp.exp(idx * lg_b)
    qd_f = jnp.exp((idx + 1.0) * lg_f)
    qd_b = jnp.exp((C - idx) * lg_b)
    chunk_len = jnp.full((1, 1), C, F32)
    cd_f = jnp.exp(chunk_len * lg_f)
    cd_b = jnp.exp(chunk_len * lg_b)

    row = lax.broadcasted_iota(jnp.int32, (C, C), 0)
    col = lax.broadcasted_iota(jnp.int32, (C, C), 1)
    decay_ref[...] = jnp.exp(jnp.where(row >= col, (row - col).astype(F32) * lg_f,
                                       (col - row).astype(F32) * lg_b))

    def rows(n):
        return pl.ds(pl.multiple_of(n * C, C), C)

    sf_ref[...] = jnp.zeros_like(sf_ref)
    sb_ref[...] = jnp.zeros_like(sb_ref)

    def forward_state(n, carry):
        k = k_ref[rows(n), :].astype(F32)
        sf_all_ref[n] = sf_ref[...].astype(BF16)
        sf_ref[...] = sf_ref[...] * cd_f + _dot_tn((k * kd_f).astype(BF16), v_ref[rows(n), :])
        return carry

    lax.fori_loop(0, N, forward_state, 0, unroll=2)

    def output_chunk(i, carry):
        n = N - 1 - i
        q = q_ref[rows(n), :]
        k = k_ref[rows(n), :]
        v = v_ref[rows(n), :]
        qf = q.astype(F32)
        scores = (_dot_nt(q, k) * decay_ref[...]).astype(BF16)
        o = (_dot(scores, v)
             + _dot((qf * qd_f).astype(BF16), sf_all_ref[n])
             + _dot((qf * qd_b).astype(BF16), sb_ref[...].astype(BF16)))
        mu = jnp.mean(o, axis=-1, keepdims=True)
        d = o - mu
        var = jnp.mean(d * d, axis=-1, keepdims=True)
        o = d * lax.rsqrt(var + GN_EPS)
        g = g_ref[rows(n), :].astype(F32)
        y_ref[rows(n), :] = (g * jax.nn.sigmoid(g) * o).astype(y_ref.dtype)
        sb_ref[...] = sb_ref[...] * cd_b + _dot_tn((k.astype(F32) * kd_b).astype(BF16), v)
        return carry

    lax.fori_loop(0, N, output_chunk, 0, unroll=4)


def retention_core(proj, lg_f, lg_b, B, T):
    H, dk, dv = RET_HEADS, RET_DK, RET_DV
    C = min(RET_CHUNK, T)
    N = T // C
    v_blk0 = 2 * H * dk // dv
    g_blk0 = v_blk0 + H
    smem = pl.BlockSpec(memory_space=pltpu.SMEM)
    return pl.pallas_call(
        functools.partial(_retention_body, C=C),
        grid=(B, H),
        in_specs=[
            smem, smem,
            pl.BlockSpec((T, dk), lambda b, h: (b, h)),
            pl.BlockSpec((T, dk), lambda b, h: (b, H + h)),
            pl.BlockSpec((T, dv), lambda b, h: (b, v_blk0 + h)),
            pl.BlockSpec((T, dv), lambda b, h: (b, g_blk0 + h)),
        ],
        out_specs=pl.BlockSpec((T, dv), lambda b, h: (b, h)),
        out_shape=jax.ShapeDtypeStruct((B * T, H * dv), BF16),
        scratch_shapes=[pltpu.VMEM((C, C), F32), pltpu.VMEM((N, dk, dv), BF16),
                        pltpu.VMEM((dk, dv), F32), pltpu.VMEM((dk, dv), F32)],
        compiler_params=_cparams("parallel", "parallel"),
        name="retention",
    )(lg_f, lg_b, proj, proj, proj, proj)


def retention_mixer(h, xn, w_in, decay_fwd, decay_bwd, w_out, next_gain, B, T):
    proj = ret_in_proj(xn, w_in.astype(BF16), T)
    lg_f = jax.nn.log_sigmoid(decay_fwd.astype(F32))
    lg_b = jax.nn.log_sigmoid(decay_bwd.astype(F32))
    y = retention_core(proj, lg_f, lg_b, B, T)
    return matmul_residual(y, w_out.astype(BF16), h, next_gain)


def _na_block_kinds(rows):
    return ((0, 0), (NA_QROWS, 0), (rows - NA_QROWS, rows - NA_KROWS))


def _na_row_pair(rows, r0, ks, a, bb):
    rq, rk = r0 + a, ks + bb
    r_start = min(max(rq - NA_WIN_R // 2, 0), rows - NA_WIN_R)
    return r_start <= rk < r_start + NA_WIN_R, rk - rq + NA_WIN_R - 1


def _na_build_bias(rpb_ref, tile_ref, bias_ref, h, rows):
    W, R, KR = GRID_W, NA_QROWS, NA_KROWS
    n_dr, n_dc = 2 * NA_WIN_R - 1, 2 * NA_WIN_C - 1
    lane = lax.broadcasted_iota(jnp.int32, (W, 2 * W), 1)
    cq = lax.broadcasted_iota(jnp.int32, (W, 2 * W), 0)
    ck = lane & (W - 1)
    hi = lane >= W
    dc = ck - cq + NA_WIN_C - 1
    c_start = jnp.clip(cq - NA_WIN_C // 2, 0, W - NA_WIN_C)
    col_ok = (ck >= c_start) & (ck < c_start + NA_WIN_C)
    base = h * (n_dr * n_dc)
    for t in range(n_dr):
        acc = jnp.zeros((W, 2 * W), F32)
        for j in range(n_dc):
            lo = rpb_ref[base + t * n_dc + j]
            hv = rpb_ref[base + (t + 1) * n_dc + j] if t + 1 < n_dr else 0.0
            acc = jnp.where(dc == j, jnp.where(hi, hv, lo), acc)
        tile_ref[t] = jnp.where(col_ok, acc, MASK_NEG)
    for c, (r0, ks) in enumerate(_na_block_kinds(rows)):
        for a in range(R):
            for pp in range(KR // 2):
                ok0, dr0 = _na_row_pair(rows, r0, ks, a, 2 * pp)
                ok1, dr1 = _na_row_pair(rows, r0, ks, a, 2 * pp + 1)
                if ok0 or ok1:
                    assert 0 <= dr0 < n_dr and (dr1 < n_dr or not ok1)
                    tile = tile_ref[dr0]
                    if not ok0:
                        tile = jnp.where(hi, tile, MASK_NEG)
                    if not ok1:
                        tile = jnp.where(hi, MASK_NEG, tile)
                else:
                    tile = jnp.full((W, 2 * W), MASK_NEG, F32)
                bias_ref[c, a * W:(a + 1) * W, pp * 2 * W:(pp + 1) * 2 * W] = tile


def _na_body(rpb_ref, q_ref, k_ref, v_ref, o_ref, tile_ref, bias_ref, *, rows):
    W, R, KR = GRID_W, NA_QROWS, NA_KROWS
    nb = rows // R
    tq, tk = R * W, KR * W

    @pl.when(pl.program_id(1) == 0)
    def _():
        _na_build_bias(rpb_ref, tile_ref, bias_ref, pl.program_id(0), rows)

    def key_rows(i):
        k0 = min(max(R * i - R, 0), rows - KR) * W
        return slice(k0, k0 + tk)

    def scores(i):
        kind = 0 if i == 0 else (2 if i == nb - 1 else 1)
        return _dot_nt(q_ref[i * tq:(i + 1) * tq, :], k_ref[key_rows(i), :]) + bias_ref[kind]

    s = scores(0)
    for i in range(nb):
        s_next = scores(i + 1) if i + 1 < nb else None
        m = jnp.max(s, axis=-1, keepdims=True)
        p = jnp.exp(s - m)
        l = jnp.sum(p, axis=-1, keepdims=True)
        o = _dot(p.astype(BF16), v_ref[key_rows(i), :]) / l
        o_ref[i * tq:(i + 1) * tq, :] = o.astype(o_ref.dtype)
        s = s_next


def na_core(qkv, rpb, B, T):
    H, dh, W = NA_HEADS, NA_DH, GRID_W
    rows = T // W
    tq, tk = NA_QROWS * W, NA_KROWS * W
    return pl.pallas_call(
        functools.partial(_na_body, rows=rows),
        grid=(H, B),
        in_specs=[
            pl.BlockSpec(memory_space=pltpu.SMEM),
            pl.BlockSpec((T, dh), lambda h, b: (b, h)),
            pl.BlockSpec((T, dh), lambda h, b: (b, H + h)),
            pl.BlockSpec((T, dh), lambda h, b: (b, 2 * H + h)),
        ],
        out_specs=pl.BlockSpec((T, dh), lambda h, b: (b, h)),
        out_shape=jax.ShapeDtypeStruct((B * T, H * dh), BF16),
        scratch_shapes=[pltpu.VMEM((2 * NA_WIN_R - 1, W, 2 * W), F32), pltpu.VMEM((3, tq, tk), F32)],
        compiler_params=_cparams("parallel", "arbitrary"),
        name="neighbourhood_attention",
    )(rpb.astype(F32).reshape(-1), qkv, qkv, qkv)


def na_mixer(h, xn, w_qkv, rpb, w_out, next_gain, B, T):
    H, dh = NA_HEADS, NA_DH
    rows = T // GRID_W
    assert rows % NA_QROWS == 0 and rows >= NA_KROWS and NA_KROWS % 2 == 0
    col_scale = jnp.concatenate([jnp.full((H * dh,), dh ** -0.5, F32), jnp.ones((2 * H * dh,), F32)])
    qkv = matmul(xn, w_qkv.astype(BF16), col_scale)
    o = na_core(qkv, rpb, B, T)
    return matmul_residual(o, w_out.astype(BF16), h, next_gain)


def _mla_down_body(x_ref, wq_ref, wkv_ref, wr_ref, wrs_ref, qn_ref, kvn_ref,
                   cos_ref, sin_ref, cq_ref, ckv_ref, kr_ref):
    xn = x_ref[...]
    cq_ref[...] = _rms(_dot(xn, wq_ref[...]), qn_ref[...]).astype(cq_ref.dtype)
    ckv_ref[...] = _rms(_dot(xn, wkv_ref[...]), kvn_ref[...]).astype(ckv_ref.dtype)
    kr = _dot(xn, wr_ref[...]) * cos_ref[...] + _dot(xn, wrs_ref[...]) * sin_ref[...]
    kr_ref[...] = kr.astype(kr_ref.dtype)


def _swap_halves(w, group):
    K, N = w.shape
    w = w.reshape(K, N // group, 2, group // 2)
    return w[:, :, ::-1, :].reshape(K, N)


def mla_down(xn, w_down, q_norm, kv_norm, T, tm=1024):
    M, D = xn.shape
    tm = _pick(T, tm)
    nT = T // tm
    w = w_down.astype(BF16)
    qr, kvr, rp = MLA_Q_RANK, MLA_KV_RANK, MLA_ROPE
    w_q, w_kv, w_r = w[:, :qr], w[:, qr:qr + kvr], w[:, qr + kvr:]
    pad = jnp.zeros((D, 128 - rp), BF16)
    w_rs = jnp.concatenate([_swap_halves(w_r, rp), pad], axis=1)
    w_r = jnp.concatenate([w_r, pad], axis=1)
    cos, sin = _rope_tables(T, rp)
    zpad = jnp.zeros((T, 128 - rp), F32)
    cos_t = jnp.concatenate([cos, cos, zpad], axis=1)
    sin_t = jnp.concatenate([-sin, sin, zpad], axis=1)
    full = lambda shape: pl.BlockSpec(shape, lambda i: (0, 0))
    return pl.pallas_call(
        _mla_down_body,
        grid=(M // tm,),
        in_specs=[
            pl.BlockSpec((tm, D), lambda i: (i, 0)),
            full((D, qr)), full((D, kvr)), full((D, 128)), full((D, 128)),
            full((1, qr)), full((1, kvr)),
            pl.BlockSpec((tm, 128), lambda i: (i % nT, 0)),
            pl.BlockSpec((tm, 128), lambda i: (i % nT, 0)),
        ],
        out_specs=[
            pl.BlockSpec((tm, qr), lambda i: (i, 0)),
            pl.BlockSpec((tm, kvr), lambda i: (i, 0)),
            pl.BlockSpec((tm, 128), lambda i: (i, 0)),
        ],
        out_shape=[
            jax.ShapeDtypeStruct((M, qr), BF16),
            jax.ShapeDtypeStruct((M, kvr), BF16),
            jax.ShapeDtypeStruct((M, 128), BF16),
        ],
        compiler_params=_cparams("parallel"),
        name="mla_down",
    )(xn, w_q, w_kv, w_r, w_rs, q_norm.reshape(1, qr), kv_norm.reshape(1, kvr), cos_t, sin_t)


def _mla_q_body(c_ref, w_ref, ws_ref, cos_ref, sin_ref, o_ref):
    c = c_ref[...]
    P = MLA_QK_PAD
    scale = (MLA_NOPE + MLA_ROPE) ** -0.5 * LOG2_E
    cos = cos_ref[...]
    sin = sin_ref[...]
    for j in range(o_ref.shape[1] // P):
        sl = slice(j * P, (j + 1) * P)
        x = _dot(c, w_ref[:, sl])
        xs = _dot(c, ws_ref[:, sl])
        o_ref[:, sl] = ((x * cos + xs * sin) * scale).astype(o_ref.dtype)


def mla_q(c_q, w_uq, T, tm=1024, heads_per_step=4):
    M, R = c_q.shape
    H, P, nope, rp = MLA_HEADS, MLA_QK_PAD, MLA_NOPE, MLA_ROPE
    tm = _pick(T, tm)
    nT = T // tm
    w = w_uq.astype(BF16).reshape(R, H, nope + rp)
    zeros = lambda n: jnp.zeros((R, H, n), BF16)
    w_rope = w[:, :, nope:]
    w_main = jnp.concatenate([w[:, :, :nope], w_rope, zeros(P - nope - rp)], axis=2).reshape(R, H * P)
    w_rope_sw = _swap_halves(w_rope.reshape(R, H * rp), rp).reshape(R, H, rp)
    w_swap = jnp.concatenate([zeros(nope), w_rope_sw, zeros(P - nope - rp)], axis=2).reshape(R, H * P)
    cos, sin = _rope_tables(T, rp)
    cos_t = jnp.concatenate([jnp.ones((T, nope), F32), cos, cos, jnp.zeros((T, P - nope - rp), F32)], axis=1)
    sin_t = jnp.concatenate([jnp.zeros((T, nope), F32), -sin, sin, jnp.zeros((T, P - nope - rp), F32)], axis=1)
    tn = heads_per_step * P
    return pl.pallas_call(
        _mla_q_body,
        grid=(M // tm, H * P // tn),
        in_specs=[
            pl.BlockSpec((tm, R), lambda i, j: (i, 0)),
            pl.BlockSpec((R, tn), lambda i, j: (0, j)),
            pl.BlockSpec((R, tn), lambda i, j: (0, j)),
            pl.BlockSpec((tm, P), lambda i, j: (i % nT, 0)),
            pl.BlockSpec((tm, P), lambda i, j: (i % nT, 0)),
        ],
        out_specs=pl.BlockSpec((tm, tn), lambda i, j: (i, j)),
        out_shape=jax.ShapeDtypeStruct((M, H * P), BF16),
        compiler_params=_cparams("parallel", "parallel"),
        name="mla_q",
    )(c_q, w_main, w_swap, cos_t, sin_t)


def _mla_attn_body(q_ref, kn_ref, kr_ref, v_ref, o_ref, kcat_ref, *, tk):
    T = kn_ref.shape[0]

    @pl.when(pl.program_id(2) == 0)
    def _():
        kcat_ref[:, :MLA_NOPE] = kn_ref[...]
        kcat_ref[:, MLA_NOPE:] = kr_ref[...]

    q = q_ref[...]
    tq = q.shape[0]
    nk = T // tk

    def scores(j):
        return _dot_nt(q, kcat_ref[j * tk:(j + 1) * tk, :])

    m = jnp.full((tq, 1), -jnp.inf, F32)
    l = jnp.zeros((tq, 1), F32)
    acc = jnp.zeros((tq, MLA_V), F32)
    s = scores(0)
    for j in range(nk):
        s_next = scores(j + 1) if j + 1 < nk else None
        m_new = jnp.maximum(m, jnp.max(s, axis=-1, keepdims=True))
        alpha = jnp.exp2(m - m_new)
        p = jnp.exp2(s - m_new)
        l = alpha * l + jnp.sum(p, axis=-1, keepdims=True)
        acc = alpha * acc + _dot(p.astype(BF16), v_ref[j * tk:(j + 1) * tk, :])
        m, s = m_new, s_next
    o_ref[...] = (acc / l).astype(o_ref.dtype)


def mla_attention(q, kv, k_rope, B, T):
    H, P = MLA_HEADS, MLA_QK_PAD
    tq, tk = _pick(T, MLA_TQ), _pick(T, MLA_TK)
    nq = T // tq
    return pl.pallas_call(
        functools.partial(_mla_attn_body, tk=tk),
        grid=(B, H, nq),
        in_specs=[
            pl.BlockSpec((tq, P), lambda b, h, i: (b * nq + i, h)),
            pl.BlockSpec((T, MLA_NOPE), lambda b, h, i: (b, h)),
            pl.BlockSpec((T, 128), lambda b, h, i: (b, 0)),
            pl.BlockSpec((T, MLA_V), lambda b, h, i: (b, H + h)),
        ],
        out_specs=pl.BlockSpec((tq, MLA_V), lambda b, h, i: (b * nq + i, h)),
        out_shape=jax.ShapeDtypeStruct((B * T, H * MLA_V), BF16),
        scratch_shapes=[pltpu.VMEM((T, P), BF16)],
        compiler_params=_cparams("parallel", "parallel", "arbitrary"),
        name="mla_attention",
    )(q, kv, k_rope, kv)


def mla_mixer(h, xn, w_down, q_norm, kv_norm, w_uq, w_ukv, w_out, next_gain, B, T):
    H = MLA_HEADS
    c_q, c_kv, k_rope = mla_down(xn, w_down, q_norm, kv_norm, T)
    q = mla_q(c_q, w_uq, T)
    w = w_ukv.astype(BF16).reshape(MLA_KV_RANK, H, MLA_NOPE + MLA_V)
    w_kv = jnp.concatenate([w[:, :, :MLA_NOPE].reshape(MLA_KV_RANK, H * MLA_NOPE),
                            w[:, :, MLA_NOPE:].reshape(MLA_KV_RANK, H * MLA_V)], axis=1)
    kv = matmul(c_kv, w_kv)
    o = mla_attention(q, kv, k_rope, B, T)
    return matmul_residual(o, w_out.astype(BF16), h, next_gain)


def _ffn_up_body(x_ref, wg_ref, wv_ref, cwg_ref, cwv_ref, bg_ref, bv_ref, o_ref, ug_ref, uv_ref, *, rc):
    T, tn = o_ref.shape
    n = T // rc
    first = lax.broadcasted_iota(jnp.int32, (rc, 1), 0) == 0
    last = lax.broadcasted_iota(jnp.int32, (rc, 1), 0) == rc - 1
    zero = jnp.zeros((1, tn), F32)

    def project(c):
        x = x_ref[c * rc:(c + 1) * rc, :]
        ug_ref[c % 2] = _dot(x, wg_ref[...])
        uv_ref[c % 2] = _dot(x, wv_ref[...])

    def conv(u, before, after, w_ref, b_ref):
        prev = jnp.where(first, before, pltpu.roll(u, 1, axis=0))
        nxt = jnp.where(last, after, pltpu.roll(u, rc - 1, axis=0))
        return prev * w_ref[0:1, :] + u * w_ref[1:2, :] + nxt * w_ref[2:3, :] + b_ref[...]

    project(0)
    before = (zero, zero)
    for c in range(n):
        if c + 1 < n:
            project(c + 1)
            after = (ug_ref[(c + 1) % 2, 0:1, :], uv_ref[(c + 1) % 2, 0:1, :])
        else:
            after = (zero, zero)
        ug = ug_ref[c % 2]
        uv = uv_ref[c % 2]
        g = conv(ug, before[0], after[0], cwg_ref, bg_ref)
        val = conv(uv, before[1], after[1], cwv_ref, bv_ref)
        o_ref[c * rc:(c + 1) * rc, :] = (g * jax.nn.sigmoid(g) * val).astype(o_ref.dtype)
        before = (ug[rc - 1:rc, :], uv[rc - 1:rc, :])


def ffn_up_conv_gate(xn, w_up, conv_w, conv_b, B, T, tn=512, rc=512):
    D = xn.shape[1]
    F = w_up.shape[1] // 2
    tn, rc = _pick(F, tn), _pick(T, rc)
    nj = F // tn
    b2 = conv_b.reshape(1, 2 * F).astype(F32)
    cw = conv_w.astype(F32)
    return pl.pallas_call(
        functools.partial(_ffn_up_body, rc=rc),
        grid=(B, nj),
        in_specs=[
            pl.BlockSpec((T, D), lambda b, j: (b, 0), pipeline_mode=pl.Buffered(1)),
            pl.BlockSpec((D, tn), lambda b, j: (0, j)),
            pl.BlockSpec((D, tn), lambda b, j: (0, nj + j)),
            pl.BlockSpec((CONV_W, tn), lambda b, j: (0, j)),
            pl.BlockSpec((CONV_W, tn), lambda b, j: (0, nj + j)),
            pl.BlockSpec((1, tn), lambda b, j: (0, j)),
            pl.BlockSpec((1, tn), lambda b, j: (0, nj + j)),
        ],
        out_specs=pl.BlockSpec((T, tn), lambda b, j: (b, j)),
        out_shape=jax.ShapeDtypeStruct((B * T, F), BF16),
        scratch_shapes=[pltpu.VMEM((2, rc, tn), F32), pltpu.VMEM((2, rc, tn), F32)],
        compiler_params=_cparams("parallel", "arbitrary"),
        name="ffn_up_conv_gate",
    )(xn, w_up, w_up, cw, cw, b2, b2)


def conv_ffn(h, xn, w_up, conv_w, conv_b, w_down, next_gain, B, T, final=False):
    a = ffn_up_conv_gate(xn, w_up.astype(BF16), conv_w, conv_b, B, T)
    return matmul_residual(a, w_down.astype(BF16), h, next_gain, final=final)


def kernel(x, l0_attn_norm, l0_ret_w_in, l0_ret_decay_fwd, l0_ret_decay_bwd, l0_ret_w_out, l0_ffn_norm, l0_ffn_w_up, l0_ffn_conv_w, l0_ffn_conv_b, l0_ffn_w_down, l1_attn_norm, l1_na_w_qkv, l1_na_rpb, l1_na_w_out, l1_ffn_norm, l1_ffn_w_up, l1_ffn_conv_w, l1_ffn_conv_b, l1_ffn_w_down, l2_attn_norm, l2_mla_w_down, l2_mla_q_norm, l2_mla_kv_norm, l2_mla_w_uq, l2_mla_w_ukv, l2_mla_w_out, l2_ffn_norm, l2_ffn_w_up, l2_ffn_conv_w, l2_ffn_conv_b, l2_ffn_w_down, l3_attn_norm, l3_ret_w_in, l3_ret_decay_fwd, l3_ret_decay_bwd, l3_ret_w_out, l3_ffn_norm, l3_ffn_w_up, l3_ffn_conv_w, l3_ffn_conv_b, l3_ffn_w_down, final_norm):
    B, T, D = x.shape
    h = x.reshape(B * T, D)
    xn = rmsnorm_cast(h, l0_attn_norm)
    h, xn = retention_mixer(h, xn, l0_ret_w_in, l0_ret_decay_fwd, l0_ret_decay_bwd, l0_ret_w_out, l0_ffn_norm, B, T)
    h, xn = conv_ffn(h, xn, l0_ffn_w_up, l0_ffn_conv_w, l0_ffn_conv_b, l0_ffn_w_down, l1_attn_norm, B, T)
    h, xn = na_mixer(h, xn, l1_na_w_qkv, l1_na_rpb, l1_na_w_out, l1_ffn_norm, B, T)
    h, xn = conv_ffn(h, xn, l1_ffn_w_up, l1_ffn_conv_w, l1_ffn_conv_b, l1_ffn_w_down, l2_attn_norm, B, T)
    h, xn = mla_mixer(h, xn, l2_mla_w_down, l2_mla_q_norm, l2_mla_kv_norm, l2_mla_w_uq, l2_mla_w_ukv, l2_mla_w_out,
                      l2_ffn_norm, B, T)
    h, xn = conv_ffn(h, xn, l2_ffn_w_up, l2_ffn_conv_w, l2_ffn_conv_b, l2_ffn_w_down, l3_attn_norm, B, T)
    h, xn = retention_mixer(h, xn, l3_ret_w_in, l3_ret_decay_fwd, l3_ret_decay_bwd, l3_ret_w_out, l3_ffn_norm, B, T)
    out = conv_ffn(h, xn, l3_ffn_w_up, l3_ffn_conv_w, l3_ffn_conv_b, l3_ffn_w_down, final_norm, B, T, final=True)
    return out.reshape(B, T, D)
```

```python
import functools

import jax
import jax.numpy as jnp
from jax import lax
from jax.experimental import pallas as pl
from jax.experimental.pallas import tpu as pltpu

F32 = jnp.float32
BF16 = jnp.bfloat16

GRID_W = 64
ROPE_BASE = 10000.0
NORM_EPS = 1e-6
GN_EPS = 1e-5

RET_HEADS = 8
RET_DK = 256
RET_DV = 512
RET_CHUNK = 256

NA_HEADS = 16
NA_DH = 128
NA_WIN_R = 8
NA_WIN_C = 16
NA_QROWS = 4
NA_KROWS = NA_QROWS + NA_WIN_R

MLA_HEADS = 16
MLA_Q_RANK = 512
MLA_KV_RANK = 512
MLA_NOPE = 128
MLA_ROPE = 64
MLA_V = 128
MLA_QK_PAD = 256
MLA_TQ = 2048
MLA_TK = 1024
LOG2_E = 1.4426950408889634

CONV_W = 3
MASK_NEG = -1e30
NORM_ROWS = 256

VMEM_LIMIT = 56 * 1024 * 1024


def _cparams(*sem):
    return pltpu.CompilerParams(dimension_semantics=sem, vmem_limit_bytes=VMEM_LIMIT)


def _dot(a, b):
    return jnp.dot(a, b, preferred_element_type=F32)


def _dot_nt(a, b):
    return lax.dot_general(a, b, (((1,), (1,)), ((), ())), preferred_element_type=F32)


def _dot_tn(a, b):
    return lax.dot_general(a, b, (((0,), (0,)), ((), ())), preferred_element_type=F32)


def _rms(x, g):
    return x * lax.rsqrt(jnp.mean(x * x, axis=-1, keepdims=True) + NORM_EPS) * g


def _pick(n, pref):
    if n <= pref:
        return n
    t = pref
    while n % t:
        t //= 2
    return t


def _rmsnorm_cast_body(x_ref, g_ref, o_ref):
    o_ref[...] = _rms(x_ref[...], g_ref[...]).astype(o_ref.dtype)


def rmsnorm_cast(x, g, tm=512):
    M, K = x.shape
    tm = _pick(M, tm)
    return pl.pallas_call(
        _rmsnorm_cast_body,
        grid=(M // tm,),
        in_specs=[pl.BlockSpec((tm, K), lambda i: (i, 0)), pl.BlockSpec((1, K), lambda i: (0, 0))],
        out_specs=pl.BlockSpec((tm, K), lambda i: (i, 0)),
        out_shape=jax.ShapeDtypeStruct((M, K), BF16),
        compiler_params=_cparams("parallel"),
        name="rmsnorm_cast",
    )(x, g.reshape(1, K))


def _matmul_res_body(a_ref, w_ref, r_ref, g_ref, o_ref, *xn_ref, final):
    k = pl.program_id(1)

    @pl.when(k == 0)
    def _():
        o_ref[...] = r_ref[...]

    last = pl.num_programs(1) - 1

    @pl.when(k < last)
    def _():
        o_ref[...] += _dot(a_ref[...], w_ref[...])

    @pl.when(k == last)
    def _():
        tm = o_ref.shape[0]
        rc = min(tm, NORM_ROWS)
        for r0 in range(0, tm, rc):
            rows = slice(r0, r0 + rc)
            h = o_ref[rows, :] + _dot(a_ref[rows, :], w_ref[...])
            y = _rms(h, g_ref[...])
            if final:
                o_ref[rows, :] = y
            else:
                o_ref[rows, :] = h
                xn_ref[0][rows, :] = y.astype(BF16)


def matmul_residual(a, w, r, gain, final=False, tm=1024, tk=1024):
    M, K = a.shape
    N = w.shape[1]
    tm, tk = _pick(M, tm), _pick(K, tk)
    row_block = pl.BlockSpec((tm, N), lambda i, k: (i, 0))
    out_specs, out_shape = row_block, jax.ShapeDtypeStruct((M, N), F32)
    if not final:
        out_specs, out_shape = [row_block, row_block], [out_shape, jax.ShapeDtypeStruct((M, N), BF16)]
    return pl.pallas_call(
        functools.partial(_matmul_res_body, final=final),
        grid=(M // tm, K // tk),
        in_specs=[
            pl.BlockSpec((tm, tk), lambda i, k: (i, k)),
            pl.BlockSpec((tk, N), lambda i, k: (k, 0)),
            row_block,
            pl.BlockSpec((1, N), lambda i, k: (0, 0)),
        ],
        out_specs=out_specs,
        out_shape=out_shape,
        compiler_params=_cparams("parallel", "arbitrary"),
        name="matmul_residual",
    )(a, w, r, gain.reshape(1, N))


def _matmul_body(a_ref, w_ref, cs_ref, o_ref):
    o_ref[...] = (_dot(a_ref[...], w_ref[...]) * cs_ref[...]).astype(o_ref.dtype)


def matmul(a, w, col_scale=None, tm=1024, tn=2048):
    M, K = a.shape
    N = w.shape[1]
    tm, tn = _pick(M, tm), _pick(N, tn)
    if col_scale is None:
        col_scale = jnp.ones((N,), F32)
    return pl.pallas_call(
        _matmul_body,
        grid=(M // tm, N // tn),
        in_specs=[
            pl.BlockSpec((tm, K), lambda i, j: (i, 0)),
            pl.BlockSpec((K, tn), lambda i, j: (0, j)),
            pl.BlockSpec((1, tn), lambda i, j: (0, j)),
        ],
        out_specs=pl.BlockSpec((tm, tn), lambda i, j: (i, j)),
        out_shape=jax.ShapeDtypeStruct((M, N), BF16),
        compiler_params=_cparams("parallel", "parallel"),
        name="matmul",
    )(a, w, col_scale.reshape(1, N))


def _rope_tables(T, d):
    half = d // 2
    inv = ROPE_BASE ** (-jnp.arange(half, dtype=F32) * 2.0 / d)
    ang = jnp.arange(T, dtype=F32)[:, None] * inv[None, :]
    return jnp.cos(ang), jnp.sin(ang)


def _ret_in_body(x_ref, w_ref, cos_ref, sin_ref, o_ref, *, n_q_tiles):
    j = pl.program_id(1)

    @pl.when(j < 2 * n_q_tiles)
    def _():
        cos = cos_ref[...]
        sin = sin_ref[...]
        scale = jnp.where(j >= n_q_tiles, RET_DK ** -0.5, 1.0).astype(F32)
        half = RET_DK // 2
        for c0 in range(0, o_ref.shape[1], RET_DK):
            acc = _dot(x_ref[...], w_ref[:, c0:c0 + RET_DK])
            x1 = acc[:, :half]
            x2 = acc[:, half:]
            o_ref[:, c0:c0 + half] = ((x1 * cos - x2 * sin) * scale).astype(o_ref.dtype)
            o_ref[:, c0 + half:c0 + RET_DK] = ((x1 * sin + x2 * cos) * scale).astype(o_ref.dtype)

    @pl.when(j >= 2 * n_q_tiles)
    def _():
        o_ref[...] = _dot(x_ref[...], w_ref[...]).astype(o_ref.dtype)


def ret_in_proj(x, w, T, tm=1024, tn=2048):
    M, K = x.shape
    N = w.shape[1]
    tm, tn = _pick(T, tm), _pick(N, tn)
    nT = T // tm
    qk_w = RET_HEADS * RET_DK
    assert tn % RET_DK == 0 and qk_w % tn == 0
    cos, sin = _rope_tables(T, RET_DK)
    return pl.pallas_call(
        functools.partial(_ret_in_body, n_q_tiles=qk_w // tn),
        grid=(M // tm, N // tn),
        in_specs=[
            pl.BlockSpec((tm, K), lambda i, j: (i, 0)),
            pl.BlockSpec((K, tn), lambda i, j: (0, j)),
            pl.BlockSpec((tm, RET_DK // 2), lambda i, j: (i % nT, 0)),
            pl.BlockSpec((tm, RET_DK // 2), lambda i, j: (i % nT, 0)),
        ],
        out_specs=pl.BlockSpec((tm, tn), lambda i, j: (i, j)),
        out_shape=jax.ShapeDtypeStruct((M, N), BF16),
        compiler_params=_cparams("parallel", "parallel"),
        name="ret_in_proj",
    )(x, w, cos, sin)


def _retention_body(lgf_ref, lgb_ref, q_ref, k_ref, v_ref, g_ref, y_ref,
                    decay_ref, sf_all_ref, sf_ref, sb_ref, *, C):
    h = pl.program_id(1)
    T = q_ref.shape[0]
    N = T // C
    lg_f = lgf_ref[h]
    lg_b = lgb_ref[h]

    idx = lax.broadcasted_iota(jnp.int32, (C, 1), 0).astype(F32)
    kd_f = jnp.exp((C - 1.0 - idx) * lg_f)
    kd_b = jnp.exp(idx * lg_b)
    qd_f = jnp.exp((idx + 1.0) * lg_f)
    qd_b = jnp.exp((C - idx) * lg_b)
    chunk_len = jnp.full((1, 1), C, F32)
    cd_f = jnp.exp(chunk_len * lg_f)
    cd_b = jnp.exp(chunk_len * lg_b)

    row = lax.broadcasted_iota(jnp.int32, (C, C), 0)
    col = lax.broadcasted_iota(jnp.int32, (C, C), 1)
    decay_ref[...] = jnp.exp(jnp.where(row >= col, (row - col).astype(F32) * lg_f,
                                       (col - row).astype(F32) * lg_b))

    def rows(n):
        return pl.ds(pl.multiple_of(n * C, C), C)

    sf_ref[...] = jnp.zeros_like(sf_ref)
    sb_ref[...] = jnp.zeros_like(sb_ref)

    def forward_state(n, carry):
        k = k_ref[rows(n), :].astype(F32)
        sf_all_ref[n] = sf_ref[...].astype(BF16)
        sf_ref[...] = sf_ref[...] * cd_f + _dot_tn((k * kd_f).astype(BF16), v_ref[rows(n), :])
        return carry

    lax.fori_loop(0, N, forward_state, 0, unroll=2)

    def output_chunk(i, carry):
        n = N - 1 - i
        q = q_ref[rows(n), :]
        k = k_ref[rows(n), :]
        v = v_ref[rows(n), :]
        qf = q.astype(F32)
        scores = (_dot_nt(q, k) * decay_ref[...]).astype(BF16)
        o = (_dot(scores, v)
             + _dot((qf * qd_f).astype(BF16), sf_all_ref[n])
             + _dot((qf * qd_b).astype(BF16), sb_ref[...].astype(BF16)))
        mu = jnp.mean(o, axis=-1, keepdims=True)
        d = o - mu
        var = jnp.mean(d * d, axis=-1, keepdims=True)
        o = d * lax.rsqrt(var + GN_EPS)
        g = g_ref[rows(n), :].astype(F32)
        y_ref[rows(n), :] = (g * jax.nn.sigmoid(g) * o).astype(y_ref.dtype)
        sb_ref[...] = sb_ref[...] * cd_b + _dot_tn((k.astype(F32) * kd_b).astype(BF16), v)
        return carry

    lax.fori_loop(0, N, output_chunk, 0, unroll=4)


def retention_core(proj, lg_f, lg_b, B, T):
    H, dk, dv = RET_HEADS, RET_DK, RET_DV
    C = min(RET_CHUNK, T)
    N = T // C
    v_blk0 = 2 * H * dk // dv
    g_blk0 = v_blk0 + H
    smem = pl.BlockSpec(memory_space=pltpu.SMEM)
    return pl.pallas_call(
        functools.partial(_retention_body, C=C),
        grid=(B, H),
        in_specs=[
            smem, smem,
            pl.BlockSpec((T, dk), lambda b, h: (b, h)),
            pl.BlockSpec((T, dk), lambda b, h: (b, H + h)),
            pl.BlockSpec((T, dv), lambda b, h: (b, v_blk0 + h)),
            pl.BlockSpec((T, dv), lambda b, h: (b, g_blk0 + h)),
        ],
        out_specs=pl.BlockSpec((T, dv), lambda b, h: (b, h)),
        out_shape=jax.ShapeDtypeStruct((B * T, H * dv), BF16),
        scratch_shapes=[pltpu.VMEM((C, C), F32), pltpu.VMEM((N, dk, dv), BF16),
                        pltpu.VMEM((dk, dv), F32), pltpu.VMEM((dk, dv), F32)],
        compiler_params=_cparams("parallel", "parallel"),
        name="retention",
    )(lg_f, lg_b, proj, proj, proj, proj)


def retention_mixer(h, xn, w_in, decay_fwd, decay_bwd, w_out, next_gain, B, T):
    proj = ret_in_proj(xn, w_in.astype(BF16), T)
    lg_f = jax.nn.log_sigmoid(decay_fwd.astype(F32))
    lg_b = jax.nn.log_sigmoid(decay_bwd.astype(F32))
    y = retention_core(proj, lg_f, lg_b, B, T)
    return matmul_residual(y, w_out.astype(BF16), h, next_gain)


def _na_block_kinds(rows):
    return ((0, 0), (NA_QROWS, 0), (rows - NA_QROWS, rows - NA_KROWS))


def _na_row_pair(rows, r0, ks, a, bb):
    rq, rk = r0 + a, ks + bb
    r_start = min(max(rq - NA_WIN_R // 2, 0), rows - NA_WIN_R)
    return r_start <= rk < r_start + NA_WIN_R, rk - rq + NA_WIN_R - 1


def _na_build_bias(rpb_ref, tile_ref, bias_ref, h, rows):
    W, R, KR = GRID_W, NA_QROWS, NA_KROWS
    n_dr, n_dc = 2 * NA_WIN_R - 1, 2 * NA_WIN_C - 1
    lane = lax.broadcasted_iota(jnp.int32, (W, 2 * W), 1)
    cq = lax.broadcasted_iota(jnp.int32, (W, 2 * W), 0)
    ck = lane & (W - 1)
    hi = lane >= W
    dc = ck - cq + NA_WIN_C - 1
    c_start = jnp.clip(cq - NA_WIN_C // 2, 0, W - NA_WIN_C)
    col_ok = (ck >= c_start) & (ck < c_start + NA_WIN_C)
    base = h * (n_dr * n_dc)
    for t in range(n_dr):
        acc = jnp.zeros((W, 2 * W), F32)
        for j in range(n_dc):
            lo = rpb_ref[base + t * n_dc + j]
            hv = rpb_ref[base + (t + 1) * n_dc + j] if t + 1 < n_dr else 0.0
            acc = jnp.where(dc == j, jnp.where(hi, hv, lo), acc)
        tile_ref[t] = jnp.where(col_ok, acc, MASK_NEG)
    for c, (r0, ks) in enumerate(_na_block_kinds(rows)):
        for a in range(R):
            for pp in range(KR // 2):
                ok0, dr0 = _na_row_pair(rows, r0, ks, a, 2 * pp)
                ok1, dr1 = _na_row_pair(rows, r0, ks, a, 2 * pp + 1)
                if ok0 or ok1:
                    assert 0 <= dr0 < n_dr and (dr1 < n_dr or not ok1)
                    tile = tile_ref[dr0]
                    if not ok0:
                        tile = jnp.where(hi, tile, MASK_NEG)
                    if not ok1:
                        tile = jnp.where(hi, MASK_NEG, tile)
                else:
                    tile = jnp.full((W, 2 * W), MASK_NEG, F32)
                bias_ref[c, a * W:(a + 1) * W, pp * 2 * W:(pp + 1) * 2 * W] = tile


def _na_body(rpb_ref, q_ref, k_ref, v_ref, o_ref, tile_ref, bias_ref, *, rows):
    W, R, KR = GRID_W, NA_QROWS, NA_KROWS
    nb = rows // R
    tq, tk = R * W, KR * W

    @pl.when(pl.program_id(1) == 0)
    def _():
        _na_build_bias(rpb_ref, tile_ref, bias_ref, pl.program_id(0), rows)

    def key_rows(i):
        k0 = min(max(R * i - R, 0), rows - KR) * W
        return slice(k0, k0 + tk)

    def scores(i):
        kind = 0 if i == 0 else (2 if i == nb - 1 else 1)
        return _dot_nt(q_ref[i * tq:(i + 1) * tq, :], k_ref[key_rows(i), :]) + bias_ref[kind]

    s = scores(0)
    for i in range(nb):
        s_next = scores(i + 1) if i + 1 < nb else None
        m = jnp.max(s, axis=-1, keepdims=True)
        p = jnp.exp(s - m)
        l = jnp.sum(p, axis=-1, keepdims=True)
        o = _dot(p.astype(BF16), v_ref[key_rows(i), :]) / l
        o_ref[i * tq:(i + 1) * tq, :] = o.astype(o_ref.dtype)
        s = s_next


def na_core(qkv, rpb, B, T):
    H, dh, W = NA_HEADS, NA_DH, GRID_W
    rows = T // W
    tq, tk = NA_QROWS * W, NA_KROWS * W
    return pl.pallas_call(
        functools.partial(_na_body, rows=rows),
        grid=(H, B),
        in_specs=[
            pl.BlockSpec(memory_space=pltpu.SMEM),
            pl.BlockSpec((T, dh), lambda h, b: (b, h)),
            pl.BlockSpec((T, dh), lambda h, b: (b, H + h)),
            pl.BlockSpec((T, dh), lambda h, b: (b, 2 * H + h)),
        ],
        out_specs=pl.BlockSpec((T, dh), lambda h, b: (b, h)),
        out_shape=jax.ShapeDtypeStruct((B * T, H * dh), BF16),
        scratch_shapes=[pltpu.VMEM((2 * NA_WIN_R - 1, W, 2 * W), F32), pltpu.VMEM((3, tq, tk), F32)],
        compiler_params=_cparams("parallel", "arbitrary"),
        name="neighbourhood_attention",
    )(rpb.astype(F32).reshape(-1), qkv, qkv, qkv)


def na_mixer(h, xn, w_qkv, rpb, w_out, next_gain, B, T):
    H, dh = NA_HEADS, NA_DH
    rows = T // GRID_W
    assert rows % NA_QROWS == 0 and rows >= NA_KROWS and NA_KROWS % 2 == 0
    col_scale = jnp.concatenate([jnp.full((H * dh,), dh ** -0.5, F32), jnp.ones((2 * H * dh,), F32)])
    qkv = matmul(xn, w_qkv.astype(BF16), col_scale)
    o = na_core(qkv, rpb, B, T)
    return matmul_residual(o, w_out.astype(BF16), h, next_gain)


def _mla_down_body(x_ref, wq_ref, wkv_ref, wr_ref, wrs_ref, qn_ref, kvn_ref,
                   cos_ref, sin_ref, cq_ref, ckv_ref, kr_ref):
    xn = x_ref[...]
    cq_ref[...] = _rms(_dot(xn, wq_ref[...]), qn_ref[...]).astype(cq_ref.dtype)
    ckv_ref[...] = _rms(_dot(xn, wkv_ref[...]), kvn_ref[...]).astype(ckv_ref.dtype)
    kr = _dot(xn, wr_ref[...]) * cos_ref[...] + _dot(xn, wrs_ref[...]) * sin_ref[...]
    kr_ref[...] = kr.astype(kr_ref.dtype)


def _swap_halves(w, group):
    K, N = w.shape
    w = w.reshape(K, N // group, 2, group // 2)
    return w[:, :, ::-1, :].reshape(K, N)


def mla_down(xn, w_down, q_norm, kv_norm, T, tm=1024):
    M, D = xn.shape
    tm = _pick(T, tm)
    nT = T // tm
    w = w_down.astype(BF16)
    qr, kvr, rp = MLA_Q_RANK, MLA_KV_RANK, MLA_ROPE
    w_q, w_kv, w_r = w[:, :qr], w[:, qr:qr + kvr], w[:, qr + kvr:]
    pad = jnp.zeros((D, 128 - rp), BF16)
    w_rs = jnp.concatenate([_swap_halves(w_r, rp), pad], axis=1)
    w_r = jnp.concatenate([w_r, pad], axis=1)
    cos, sin = _rope_tables(T, rp)
    zpad = jnp.zeros((T, 128 - rp), F32)
    cos_t = jnp.concatenate([cos, cos, zpad], axis=1)
    sin_t = jnp.concatenate([-sin, sin, zpad], axis=1)
    full = lambda shape: pl.BlockSpec(shape, lambda i: (0, 0))
    return pl.pallas_call(
        _mla_down_body,
        grid=(M // tm,),
        in_specs=[
            pl.BlockSpec((tm, D), lambda i: (i, 0)),
            full((D, qr)), full((D, kvr)), full((D, 128)), full((D, 128)),
            full((1, qr)), full((1, kvr)),
            pl.BlockSpec((tm, 128), lambda i: (i % nT, 0)),
            pl.BlockSpec((tm, 128), lambda i: (i % nT, 0)),
        ],
        out_specs=[
            pl.BlockSpec((tm, qr), lambda i: (i, 0)),
            pl.BlockSpec((tm, kvr), lambda i: (i, 0)),
            pl.BlockSpec((tm, 128), lambda i: (i, 0)),
        ],
        out_shape=[
            jax.ShapeDtypeStruct((M, qr), BF16),
            jax.ShapeDtypeStruct((M, kvr), BF16),
            jax.ShapeDtypeStruct((M, 128), BF16),
        ],
        compiler_params=_cparams("parallel"),
        name="mla_down",
    )(xn, w_q, w_kv, w_r, w_rs, q_norm.reshape(1, qr), kv_norm.reshape(1, kvr), cos_t, sin_t)


def _mla_q_body(c_ref, w_ref, ws_ref, cos_ref, sin_ref, o_ref):
    c = c_ref[...]
    P = MLA_QK_PAD
    scale = (MLA_NOPE + MLA_ROPE) ** -0.5 * LOG2_E
    cos = cos_ref[...]
    sin = sin_ref[...]
    for j in range(o_ref.shape[1] // P):
        sl = slice(j * P, (j + 1) * P)
        x = _dot(c, w_ref[:, sl])
        xs = _dot(c, ws_ref[:, sl])
        o_ref[:, sl] = ((x * cos + xs * sin) * scale).astype(o_ref.dtype)


def mla_q(c_q, w_uq, T, tm=1024, heads_per_step=4):
    M, R = c_q.shape
    H, P, nope, rp = MLA_HEADS, MLA_QK_PAD, MLA_NOPE, MLA_ROPE
    tm = _pick(T, tm)
    nT = T // tm
    w = w_uq.astype(BF16).reshape(R, H, nope + rp)
    zeros = lambda n: jnp.zeros((R, H, n), BF16)
    w_rope = w[:, :, nope:]
    w_main = jnp.concatenate([w[:, :, :nope], w_rope, zeros(P - nope - rp)], axis=2).reshape(R, H * P)
    w_rope_sw = _swap_halves(w_rope.reshape(R, H * rp), rp).reshape(R, H, rp)
    w_swap = jnp.concatenate([zeros(nope), w_rope_sw, zeros(P - nope - rp)], axis=2).reshape(R, H * P)
    cos, sin = _rope_tables(T, rp)
    cos_t = jnp.concatenate([jnp.ones((T, nope), F32), cos, cos, jnp.zeros((T, P - nope - rp), F32)], axis=1)
    sin_t = jnp.concatenate([jnp.zeros((T, nope), F32), -sin, sin, jnp.zeros((T, P - nope - rp), F32)], axis=1)
    tn = heads_per_step * P
    return pl.pallas_call(
        _mla_q_body,
        grid=(M // tm, H * P // tn),
        in_specs=[
            pl.BlockSpec((tm, R), lambda i, j: (i, 0)),
            pl.BlockSpec((R, tn), lambda i, j: (0, j)),
            pl.BlockSpec((R, tn), lambda i, j: (0, j)),
            pl.BlockSpec((tm, P), lambda i, j: (i % nT, 0)),
            pl.BlockSpec((tm, P), lambda i, j: (i % nT, 0)),
        ],
        out_specs=pl.BlockSpec((tm, tn), lambda i, j: (i, j)),
        out_shape=jax.ShapeDtypeStruct((M, H * P), BF16),
        compiler_params=_cparams("parallel", "parallel"),
        name="mla_q",
    )(c_q, w_main, w_swap, cos_t, sin_t)


def _mla_attn_body(q_ref, kn_ref, kr_ref, v_ref, o_ref, kcat_ref, *, tk):
    T = kn_ref.shape[0]

    @pl.when(pl.program_id(2) == 0)
    def _():
        kcat_ref[:, :MLA_NOPE] = kn_ref[...]
        kcat_ref[:, MLA_NOPE:] = kr_ref[...]

    q = q_ref[...]
    tq = q.shape[0]
    nk = T // tk

    def scores(j):
        return _dot_nt(q, kcat_ref[j * tk:(j + 1) * tk, :])

    m = jnp.full((tq, 1), -jnp.inf, F32)
    l = jnp.zeros((tq, 1), F32)
    acc = jnp.zeros((tq, MLA_V), F32)
    s = scores(0)
    for j in range(nk):
        s_next = scores(j + 1) if j + 1 < nk else None
        m_new = jnp.maximum(m, jnp.max(s, axis=-1, keepdims=True))
        alpha = jnp.exp2(m - m_new)
        p = jnp.exp2(s - m_new)
        l = alpha * l + jnp.sum(p, axis=-1, keepdims=True)
        acc = alpha * acc + _dot(p.astype(BF16), v_ref[j * tk:(j + 1) * tk, :])
        m, s = m_new, s_next
    o_ref[...] = (acc / l).astype(o_ref.dtype)


def mla_attention(q, kv, k_rope, B, T):
    H, P = MLA_HEADS, MLA_QK_PAD
    tq, tk = _pick(T, MLA_TQ), _pick(T, MLA_TK)
    nq = T // tq
    return pl.pallas_call(
        functools.partial(_mla_attn_body, tk=tk),
        grid=(B, H, nq),
        in_specs=[
            pl.BlockSpec((tq, P), lambda b, h, i: (b * nq + i, h)),
            pl.BlockSpec((T, MLA_NOPE), lambda b, h, i: (b, h)),
            pl.BlockSpec((T, 128), lambda b, h, i: (b, 0)),
            pl.BlockSpec((T, MLA_V), lambda b, h, i: (b, H + h)),
        ],
        out_specs=pl.BlockSpec((tq, MLA_V), lambda b, h, i: (b * nq + i, h)),
        out_shape=jax.ShapeDtypeStruct((B * T, H * MLA_V), BF16),
        scratch_shapes=[pltpu.VMEM((T, P), BF16)],
        compiler_params=_cparams("parallel", "parallel", "arbitrary"),
        name="mla_attention",
    )(q, kv, k_rope, kv)


def mla_mixer(h, xn, w_down, q_norm, kv_norm, w_uq, w_ukv, w_out, next_gain, B, T):
    H = MLA_HEADS
    c_q, c_kv, k_rope = mla_down(xn, w_down, q_norm, kv_norm, T)
    q = mla_q(c_q, w_uq, T)
    w = w_ukv.astype(BF16).reshape(MLA_KV_RANK, H, MLA_NOPE + MLA_V)
    w_kv = jnp.concatenate([w[:, :, :MLA_NOPE].reshape(MLA_KV_RANK, H * MLA_NOPE),
                            w[:, :, MLA_NOPE:].reshape(MLA_KV_RANK, H * MLA_V)], axis=1)
    kv = matmul(c_kv, w_kv)
    o = mla_attention(q, kv, k_rope, B, T)
    return matmul_residual(o, w_out.astype(BF16), h, next_gain)


def _ffn_up_body(x_ref, wg_ref, wv_ref, cwg_ref, cwv_ref, bg_ref, bv_ref, o_ref, ug_ref, uv_ref, *, rc):
    T, tn = o_ref.shape
    n = T // rc
    first = lax.broadcasted_iota(jnp.int32, (rc, 1), 0) == 0
    last = lax.broadcasted_iota(jnp.int32, (rc, 1), 0) == rc - 1
    zero = jnp.zeros((1, tn), F32)

    def project(c):
        x = x_ref[c * rc:(c + 1) * rc, :]
        ug_ref[c % 2] = _dot(x, wg_ref[...])
        uv_ref[c % 2] = _dot(x, wv_ref[...])

    def conv(u, before, after, w_ref, b_ref):
        prev = jnp.where(first, before, pltpu.roll(u, 1, axis=0))
        nxt = jnp.where(last, after, pltpu.roll(u, rc - 1, axis=0))
        return prev * w_ref[0:1, :] + u * w_ref[1:2, :] + nxt * w_ref[2:3, :] + b_ref[...]

    project(0)
    before = (zero, zero)
    for c in range(n):
        if c + 1 < n:
            project(c + 1)
            after = (ug_ref[(c + 1) % 2, 0:1, :], uv_ref[(c + 1) % 2, 0:1, :])
        else:
            after = (zero, zero)
        ug = ug_ref[c % 2]
        uv = uv_ref[c % 2]
        g = conv(ug, before[0], after[0], cwg_ref, bg_ref)
        val = conv(uv, before[1], after[1], cwv_ref, bv_ref)
        o_ref[c * rc:(c + 1) * rc, :] = (g * jax.nn.sigmoid(g) * val).astype(o_ref.dtype)
        before = (ug[rc - 1:rc, :], uv[rc - 1:rc, :])


def ffn_up_conv_gate(xn, w_up, conv_w, conv_b, B, T, tn=512, rc=512):
    D = xn.shape[1]
    F = w_up.shape[1] // 2
    tn, rc = _pick(F, tn), _pick(T, rc)
    nj = F // tn
    b2 = conv_b.reshape(1, 2 * F).astype(F32)
    cw = conv_w.astype(F32)
    return pl.pallas_call(
        functools.partial(_ffn_up_body, rc=rc),
        grid=(B, nj),
        in_specs=[
            pl.BlockSpec((T, D), lambda b, j: (b, 0), pipeline_mode=pl.Buffered(1)),
            pl.BlockSpec((D, tn), lambda b, j: (0, j)),
            pl.BlockSpec((D, tn), lambda b, j: (0, nj + j)),
            pl.BlockSpec((CONV_W, tn), lambda b, j: (0, j)),
            pl.BlockSpec((CONV_W, tn), lambda b, j: (0, nj + j)),
            pl.BlockSpec((1, tn), lambda b, j: (0, j)),
            pl.BlockSpec((1, tn), lambda b, j: (0, nj + j)),
        ],
        out_specs=pl.BlockSpec((T, tn), lambda b, j: (b, j)),
        out_shape=jax.ShapeDtypeStruct((B * T, F), BF16),
        scratch_shapes=[pltpu.VMEM((2, rc, tn), F32), pltpu.VMEM((2, rc, tn), F32)],
        compiler_params=_cparams("parallel", "arbitrary"),
        name="ffn_up_conv_gate",
    )(xn, w_up, w_up, cw, cw, b2, b2)


def conv_ffn(h, xn, w_up, conv_w, conv_b, w_down, next_gain, B, T, final=False):
    a = ffn_up_conv_gate(xn, w_up.astype(BF16), conv_w, conv_b, B, T)
    return matmul_residual(a, w_down.astype(BF16), h, next_gain, final=final)


def kernel(x, l0_attn_norm, l0_ret_w_in, l0_ret_decay_fwd, l0_ret_decay_bwd, l0_ret_w_out, l0_ffn_norm, l0_ffn_w_up, l0_ffn_conv_w, l0_ffn_conv_b, l0_ffn_w_down, l1_attn_norm, l1_na_w_qkv, l1_na_rpb, l1_na_w_out, l1_ffn_norm, l1_ffn_w_up, l1_ffn_conv_w, l1_ffn_conv_b, l1_ffn_w_down, l2_attn_norm, l2_mla_w_down, l2_mla_q_norm, l2_mla_kv_norm, l2_mla_w_uq, l2_mla_w_ukv, l2_mla_w_out, l2_ffn_norm, l2_ffn_w_up, l2_ffn_conv_w, l2_ffn_conv_b, l2_ffn_w_down, l3_attn_norm, l3_ret_w_in, l3_ret_decay_fwd, l3_ret_decay_bwd, l3_ret_w_out, l3_ffn_norm, l3_ffn_w_up, l3_ffn_conv_w, l3_ffn_conv_b, l3_ffn_w_down, final_norm):
    B, T, D = x.shape
    h = x.reshape(B * T, D)
    xn = rmsnorm_cast(h, l0_attn_norm)
    h, xn = retention_mixer(h, xn, l0_ret_w_in, l0_ret_decay_fwd, l0_ret_decay_bwd, l0_ret_w_out, l0_ffn_norm, B, T)
    h, xn = conv_ffn(h, xn, l0_ffn_w_up, l0_ffn_conv_w, l0_ffn_conv_b, l0_ffn_w_down, l1_attn_norm, B, T)
    h, xn = na_mixer(h, xn, l1_na_w_qkv, l1_na_rpb, l1_na_w_out, l1_ffn_norm, B, T)
    h, xn = conv_ffn(h, xn, l1_ffn_w_up, l1_ffn_conv_w, l1_ffn_conv_b, l1_ffn_w_down, l2_attn_norm, B, T)
    h, xn = mla_mixer(h, xn, l2_mla_w_down, l2_mla_q_norm, l2_mla_kv_norm, l2_mla_w_uq, l2_mla_w_ukv, l2_mla_w_out,
                      l2_ffn_norm, B, T)
    h, xn = conv_ffn(h, xn, l2_ffn_w_up, l2_ffn_conv_w, l2_ffn_conv_b, l2_ffn_w_down, l3_attn_norm, B, T)
    h, xn = retention_mixer(h, xn, l3_ret_w_in, l3_ret_decay_fwd, l3_ret_decay_bwd, l3_ret_w_out, l3_ffn_norm, B, T)
    out = conv_ffn(h, xn, l3_ffn_w_up, l3_ffn_conv_w, l3_ffn_conv_b, l3_ffn_w_down, final_norm, B, T, final=True)
    return out.reshape(B, T, D)
```
